```python
import jax, jax.numpy as jnp
from jax import lax
import numpy as np

D_MODEL = 1024
BATCH = 1
SEQ = 16384
DEPTH = 4

N_MIXERS = 3
D_FF = 4 * D_MODEL
Q_BLOCK = 128
ROPE_BASE = 10000.0
EPS = 1e-6
NEG_INF = -1e30

MLA_HEADS = 8
MLA_Q_RANK = 256
MLA_KV_RANK = 128
MLA_NOPE = 128
MLA_ROPE = 64
MLA_V = 128

SB_HEADS = 16
SB_HEAD_DIM = D_MODEL // SB_HEADS

RET_HEADS = 4
RET_KEY_DIM = D_MODEL // RET_HEADS
RET_VAL_DIM = 2 * D_MODEL // RET_HEADS
RET_CHUNK = 128

N_MLA_LAYERS = (DEPTH + 2) // 3
N_SB_LAYERS = (DEPTH + 1) // 3
N_RET_LAYERS = DEPTH // 3

kernel_name = "hybrid_mla_stickbreak_retention_trunk"


def rms_norm(x, g):
    xf = x.astype(jnp.float32)
    y = xf * lax.rsqrt(jnp.mean(xf * xf, axis=-1, keepdims=True) + EPS)
    return (y * g.astype(jnp.float32)).astype(x.dtype)


def modulate(x, shift, scale):
    return x * (1.0 + scale[:, None, :]) + shift[:, None, :]


def rope_tables(positions, dim):
    inv_freq = ROPE_BASE ** (-jnp.arange(0, dim, 2, dtype=jnp.float32) / dim)
    ang = positions.astype(jnp.float32)[..., None] * inv_freq
    return jnp.cos(ang)[:, :, None, :], jnp.sin(ang)[:, :, None, :]


def apply_rope(t, cos, sin):
    t1, t2 = jnp.split(t, 2, axis=-1)
    cos = cos.astype(t.dtype)
    sin = sin.astype(t.dtype)
    return jnp.concatenate([t1 * cos - t2 * sin, t2 * cos + t1 * sin], axis=-1)


def to_blocks(t, size):
    b, s, h, d = t.shape
    return t.reshape(b, s // size, size, h, d).transpose(1, 0, 2, 3, 4)


def from_blocks(t):
    nb, b, size, h, d = t.shape
    return t.transpose(1, 0, 2, 3, 4).reshape(b, nb * size, h, d)


def blocked_causal_softmax(q, k, v, scale):
    k_pos = jnp.arange(k.shape[1])
    qb = to_blocks(q, Q_BLOCK)

    def one_block(args):
        q_blk, b_idx = args
        q_pos = b_idx * Q_BLOCK + jnp.arange(Q_BLOCK)
        s = jnp.einsum('bqhd,bkhd->bhqk', q_blk, k).astype(jnp.float32) * scale
        s = jnp.where(k_pos[None, :] <= q_pos[:, None], s, NEG_INF)
        p = jax.nn.softmax(s, axis=-1).astype(v.dtype)
        return jnp.einsum('bhqk,bkhd->bqhd', p, v)

    o = lax.map(one_block, (qb, jnp.arange(qb.shape[0])))
    return from_blocks(o)


def mla_mixer(h, positions, w_in, q_norm, kv_norm, w_q_up, w_kv_up, w_out):
    b, s, _ = h.shape
    q_lat, kv_lat, k_pe = jnp.split(h @ w_in, [MLA_Q_RANK, MLA_Q_RANK + MLA_KV_RANK], axis=-1)
    cos, sin = rope_tables(positions, MLA_ROPE)
    q = (rms_norm(q_lat, q_norm) @ w_q_up).reshape(b, s, MLA_HEADS, MLA_NOPE + MLA_ROPE)
    q_nope, q_pe = jnp.split(q, [MLA_NOPE], axis=-1)
    q = jnp.concatenate([q_nope, apply_rope(q_pe, cos, sin)], axis=-1)
    kv = (rms_norm(kv_lat, kv_norm) @ w_kv_up).reshape(b, s, MLA_HEADS, MLA_NOPE + MLA_V)
    k_nope, v = jnp.split(kv, [MLA_NOPE], axis=-1)
    k_pe = apply_rope(k_pe[:, :, None, :], cos, sin)
    k = jnp.concatenate([k_nope, jnp.broadcast_to(k_pe, (b, s, MLA_HEADS, MLA_ROPE))], axis=-1)
    o = blocked_causal_softmax(q, k, v, (MLA_NOPE + MLA_ROPE) ** -0.5)
    return o.reshape(b, s, MLA_HEADS * MLA_V) @ w_out


def stick_breaking_mixer(h, w_in, w_out):
    b, s, _ = h.shape
    q, k, v = jnp.split(h @ w_in, 3, axis=-1)
    q = q.reshape(b, s, SB_HEADS, SB_HEAD_DIM)
    k = k.reshape(b, s, SB_HEADS, SB_HEAD_DIM)
    v = v.reshape(b, s, SB_HEADS, SB_HEAD_DIM)
    scale = SB_HEAD_DIM ** -0.5
    k_pos = jnp.arange(s)
    qb = to_blocks(q, Q_BLOCK)

    def one_block(args):
        q_blk, b_idx = args
        q_pos = b_idx * Q_BLOCK + jnp.arange(Q_BLOCK)
        z = jnp.einsum('bqhd,bkhd->bhqk', q_blk, k).astype(jnp.float32) * scale
        strict = k_pos[None, :] < q_pos[:, None]
        log_1m = jnp.where(strict, jax.nn.log_sigmoid(-z), 0.0)
        suffix = lax.cumsum(log_1m, axis=3, reverse=True) - log_1m
        a = jnp.where(strict, jnp.exp(jax.nn.log_sigmoid(z) + suffix), 0.0).astype(v.dtype)
        return jnp.einsum('bhqk,bkhd->bqhd', a, v)

    o = from_blocks(lax.map(one_block, (qb, jnp.arange(qb.shape[0]))))
    return o.reshape(b, s, SB_HEADS * SB_HEAD_DIM) @ w_out


def retention_mixer(h, positions, w_in, gn_g, w_out):
    b, s, _ = h.shape
    hk = RET_HEADS * RET_KEY_DIM
    hv = RET_HEADS * RET_VAL_DIM
    q, k, v, g = jnp.split(h @ w_in, [hk, 2 * hk, 2 * hk + hv], axis=-1)
    cos, sin = rope_tables(positions, RET_KEY_DIM)
    q = apply_rope(q.reshape(b, s, RET_HEADS, RET_KEY_DIM), cos, sin).astype(jnp.float32)
    k = apply_rope(k.reshape(b, s, RET_HEADS, RET_KEY_DIM), cos, sin).astype(jnp.float32) * (RET_KEY_DIM ** -0.5)
    v = v.reshape(b, s, RET_HEADS, RET_VAL_DIM).astype(jnp.float32)

    log_gamma = jnp.log(1.0 - 2.0 ** (-5.0 - jnp.arange(RET_HEADS, dtype=jnp.float32)))
    idx = jnp.arange(RET_CHUNK, dtype=jnp.float32)
    diff = idx[:, None] - idx[None, :]
    decay = jnp.where(diff[None] >= 0, jnp.exp(jnp.maximum(diff, 0.0)[None] * log_gamma[:, None, None]), 0.0)
    xi = jnp.exp((idx[:, None] + 1.0) * log_gamma[None, :])
    zeta = jnp.exp((RET_CHUNK - 1.0 - idx)[:, None] * log_gamma[None, :])
    g_chunk = jnp.exp(RET_CHUNK * log_gamma)

    def step(state, inp):
        qc, kc, vc = inp
        sc = jnp.einsum('bihd,bjhd->bhij', qc, kc) * decay[None]
        inner = jnp.einsum('bhij,bjhv->bihv', sc, vc)
        cross = jnp.einsum('bihd,bhdv->bihv', qc, state) * xi[None, :, :, None]
        state = state * g_chunk[None, :, None, None] + jnp.einsum('bjhd,bjhv->bhdv', kc * zeta[None, :, :, None], vc)
        return state, inner + cross

    state0 = jnp.zeros((b, RET_HEADS, RET_KEY_DIM, RET_VAL_DIM), jnp.float32)
    _, o = lax.scan(step, state0, (to_blocks(q, RET_CHUNK), to_blocks(k, RET_CHUNK), to_blocks(v, RET_CHUNK)))
    o = from_blocks(o)
    o = o * lax.rsqrt(jnp.mean(o * o, axis=-1, keepdims=True) + EPS)
    o = o.reshape(b, s, hv) * gn_g.astype(jnp.float32)
    o = (jax.nn.silu(g.astype(jnp.float32)) * o).astype(h.dtype)
    return o @ w_out


def setup_inputs(seed: int = 0) -> dict:
    key = jax.random.key(seed)
    ks = jax.random.split(key, 24)

    def dense(k, shape, fan_in, gain=1.0):
        return jax.random.normal(k, shape, jnp.float32) * (gain * fan_in ** -0.5)

    def gains(k, shape):
        return 1.0 + 0.05 * jax.random.normal(k, shape, jnp.float32)

    hk = RET_HEADS * RET_KEY_DIM
    hv = RET_HEADS * RET_VAL_DIM
    return {
        "x": jax.random.normal(ks[0], (BATCH, SEQ, D_MODEL), jnp.float32),
        "c": jax.random.normal(ks[1], (BATCH, D_MODEL), jnp.float32),
        "positions": jnp.broadcast_to(jnp.arange(SEQ, dtype=jnp.int32), (BATCH, SEQ)),
        "ada_w": dense(ks[2], (DEPTH, D_MODEL, 6 * D_MODEL), D_MODEL, 0.5),
        "ada_b": 0.01 * jax.random.normal(ks[3], (DEPTH, 6 * D_MODEL), jnp.float32),
        "norm_g": gains(ks[4], (DEPTH, 4, D_MODEL)),
        "ffn_w1": dense(ks[5], (DEPTH, D_MODEL, D_FF), D_MODEL),
        "ffn_w2": dense(ks[6], (DEPTH, D_FF, D_MODEL), D_FF),
        "mla_w_in": dense(ks[7], (N_MLA_LAYERS, D_MODEL, MLA_Q_RANK + MLA_KV_RANK + MLA_ROPE), D_MODEL),
        "mla_q_norm": gains(ks[8], (N_MLA_LAYERS, MLA_Q_RANK)),
        "mla_kv_norm": gains(ks[9], (N_MLA_LAYERS, MLA_KV_RANK)),
        "mla_w_q_up": dense(ks[10], (N_MLA_LAYERS, MLA_Q_RANK, MLA_HEADS * (MLA_NOPE + MLA_ROPE)), MLA_Q_RANK),
        "mla_w_kv_up": dense(ks[11], (N_MLA_LAYERS, MLA_KV_RANK, MLA_HEADS * (MLA_NOPE + MLA_V)), MLA_KV_RANK),
        "mla_w_out": dense(ks[12], (N_MLA_LAYERS, MLA_HEADS * MLA_V, D_MODEL), MLA_HEADS * MLA_V),
        "sb_w_in": dense(ks[13], (N_SB_LAYERS, D_MODEL, 3 * SB_HEADS * SB_HEAD_DIM), D_MODEL),
        "sb_w_out": dense(ks[14], (N_SB_LAYERS, SB_HEADS * SB_HEAD_DIM, D_MODEL), SB_HEADS * SB_HEAD_DIM),
        "ret_w_in": dense(ks[15], (N_RET_LAYERS, D_MODEL, 2 * hk + 2 * hv), D_MODEL),
        "ret_gn_g": gains(ks[16], (N_RET_LAYERS, hv)),
        "ret_w_out": dense(ks[17], (N_RET_LAYERS, hv, D_MODEL), hv),
    }


def reference(x, c, positions, ada_w, ada_b, norm_g, ffn_w1, ffn_w2,
              mla_w_in, mla_q_norm, mla_kv_norm, mla_w_q_up, mla_w_kv_up, mla_w_out,
              sb_w_in, sb_w_out, ret_w_in, ret_gn_g, ret_w_out):
    cond = jax.nn.silu(c)
    for i in range(DEPTH):
        mod = cond @ ada_w[i] + ada_b[i]
        sh_a, sc_a, g_a, sh_f, sc_f, g_f = jnp.split(mod, 6, axis=-1)

        h = modulate(rms_norm(x, norm_g[i, 0]), sh_a, sc_a)
        kind, j = i % N_MIXERS, i // N_MIXERS
        if kind == 0:
            y = mla_mixer(h, positions, mla_w_in[j], mla_q_norm[j], mla_kv_norm[j],
                          mla_w_q_up[j], mla_w_kv_up[j], mla_w_out[j])
        elif kind == 1:
            y = stick_breaking_mixer(h, sb_w_in[j], sb_w_out[j])
        else:
            y = retention_mixer(h, positions, ret_w_in[j], ret_gn_g[j], ret_w_out[j])
        x = x + g_a[:, None, :] * rms_norm(y, norm_g[i, 1])

        h = modulate(rms_norm(x, norm_g[i, 2]), sh_f, sc_f)
        y = jnp.square(jax.nn.relu(h @ ffn_w1[i])) @ ffn_w2[i]
        x = x + g_f[:, None, :] * rms_norm(y, norm_g[i, 3])
    return x
```

```python
import functools
import math

import jax
import jax.numpy as jnp
from jax import lax
from jax.experimental import pallas as pl
from jax.experimental.pallas import tpu as pltpu

EPS = 1e-6
ROPE_BASE = 10000.0
MASK_VALUE = -1e30
LOG2E = math.log2(math.e)

LANES = 128
SUBLANES = 8
VMEM_LIMIT_BYTES = 60000 * 1024

MLA_HEADS = 8
MLA_Q_RANK = 256
MLA_KV_RANK = 128
MLA_NOPE = 128
MLA_ROPE = 64
MLA_V = 128
MLA_QK_PAD = 256

SB_HEADS = 16
SB_HEAD_DIM = 64

RET_HEADS = 4
RET_KEY_DIM = 256
RET_VAL_DIM = 512

ROW_TILE = 512
FF_CHUNK = 1024
MLA_BLOCK = 512
SB_BLOCK = 256
RET_CHUNK = 256

F32 = jnp.float32
BF16 = jnp.bfloat16


def _params(*semantics):
    return pltpu.CompilerParams(dimension_semantics=semantics, vmem_limit_bytes=VMEM_LIMIT_BYTES)


def _full(shape):
    nd = len(shape)
    return pl.BlockSpec(shape, lambda *_: (0,) * nd)


def _rms(x, g):
    return x * lax.rsqrt(jnp.mean(x * x, axis=-1, keepdims=True) + EPS) * g


def _dot(a, b):
    return jnp.dot(a, b, preferred_element_type=F32)


def _dot_nt(a, b):
    return lax.dot_general(a, b, (((1,), (1,)), ((), ())), preferred_element_type=F32)


def _tile_lanes(v, width):
    reps = width // LANES
    return v if reps == 1 else jnp.concatenate([v] * reps, axis=1)


def _adaln_kernel(c_ref, w_ref, b_ref, o_ref):
    c = c_ref[...]
    cond = c * (1.0 / (1.0 + jnp.exp(-c)))
    o_ref[0] = jnp.sum(w_ref[0] * cond, axis=0, keepdims=True) + b_ref[0]


def _adaln(c, ada_w, ada_b):
    depth, d, n = ada_w.shape
    tn = 768
    out = pl.pallas_call(
        _adaln_kernel,
        out_shape=jax.ShapeDtypeStruct((depth, 1, n), F32),
        grid=(depth, n // tn),
        in_specs=[
            _full((d, 1)),
            pl.BlockSpec((1, d, tn), lambda i, j: (i, 0, j)),
            pl.BlockSpec((1, 1, tn), lambda i, j: (i, 0, j)),
        ],
        out_specs=pl.BlockSpec((1, 1, tn), lambda i, j: (i, 0, j)),
        compiler_params=_params("arbitrary", "arbitrary"),
        name="adaln",
    )(c.reshape(d, 1), ada_w, ada_b.reshape(depth, 1, n))
    return out.reshape(depth, 6, d)


def _rope_kernel(pos_ref, f64_ref, f256_ref, a_mask_ref, b_sign_ref, a_ref, b_ref, cos_ref, sin_ref):
    pos = pos_ref[...]
    ang = pos * f64_ref[...]
    a_ref[...] = jnp.cos(ang) * a_mask_ref[...]
    b_ref[...] = jnp.sin(ang) * b_sign_ref[...]
    ang = pos * f256_ref[...]
    cos_ref[...] = jnp.cos(ang)
    sin_ref[...] = jnp.sin(ang)


def _rope_tables(positions):
    s = positions.shape[-1]
    tm = 1024
    pos = positions.astype(F32).reshape(s, 1)
    f64 = ROPE_BASE ** (-jnp.arange(0, MLA_ROPE, 2, dtype=F32) / MLA_ROPE)
    f256 = ROPE_BASE ** (-jnp.arange(0, RET_KEY_DIM, 2, dtype=F32) / RET_KEY_DIM)
    quarter = MLA_ROPE // 2
    ones, zeros = jnp.ones((quarter,), F32), jnp.zeros((quarter,), F32)
    a_mask = jnp.concatenate([ones, ones, zeros, zeros]).reshape(1, LANES)
    b_sign = jnp.concatenate([-ones, ones, zeros, zeros]).reshape(1, LANES)
    row = pl.BlockSpec((tm, LANES), lambda i: (i, 0))
    return pl.pallas_call(
        _rope_kernel,
        out_shape=[jax.ShapeDtypeStruct((s, LANES), F32)] * 4,
        grid=(s // tm,),
        in_specs=[pl.BlockSpec((tm, 1), lambda i: (i, 0))] + [_full((1, LANES))] * 4,
        out_specs=[row] * 4,
        compiler_params=_params("arbitrary"),
        name="rope_tables",
    )(pos, jnp.tile(f64, 4).reshape(1, LANES), f256.reshape(1, LANES), a_mask, b_sign)


def _modulated(x, g, shift, scale):
    return (_rms(x, g) * (1.0 + scale) + shift).astype(BF16)


def _mla_in_kernel(x_ref, mod_ref, ng_ref, a_ref, b_ref, w_in_ref, qn_ref, kvn_ref, wq_ref, wk_ref, wv_ref,
                   q_ref, k_ref, v_ref, *, q_scale):
    h = _modulated(x_ref[...], ng_ref[0:1, :], mod_ref[0:1, :], mod_ref[1:2, :])
    lat = _dot(h, w_in_ref[...])
    rope_a, rope_b = a_ref[...], b_ref[...]

    def rope(v):
        return v * rope_a + pltpu.roll(v, 2 * (MLA_ROPE // 2), axis=1) * rope_b

    q_lat = lat[:, :MLA_Q_RANK]
    kv_lat = lat[:, MLA_Q_RANK:MLA_Q_RANK + MLA_KV_RANK]
    k_pe = rope(lat[:, MLA_Q_RANK + MLA_KV_RANK:]).astype(BF16)
    q = _dot(_rms(q_lat, qn_ref[...]).astype(BF16), wq_ref[...])
    kvn = _rms(kv_lat, kvn_ref[...]).astype(BF16)
    k_nope = _dot(kvn, wk_ref[...]).astype(BF16)
    v_ref[...] = _dot(kvn, wv_ref[...]).astype(BF16)
    q_parts, k_parts = [], []
    for hd in range(MLA_HEADS):
        lo = hd * MLA_QK_PAD
        q_parts.append((q[:, lo:lo + MLA_NOPE] * q_scale).astype(BF16))
        q_parts.append((rope(q[:, lo + MLA_NOPE:lo + MLA_QK_PAD]) * q_scale).astype(BF16))
        k_parts.append(k_nope[:, hd * MLA_NOPE:(hd + 1) * MLA_NOPE])
        k_parts.append(k_pe)
    q_ref[...] = jnp.concatenate(q_parts, axis=1)
    k_ref[...] = jnp.concatenate(k_parts, axis=1)


def _mla_weights(w_in, w_q_up, w_kv_up):
    half = MLA_ROPE // 2
    pe = w_in[:, MLA_Q_RANK + MLA_KV_RANK:]
    pe1, pe2 = pe[:, :half], pe[:, half:]
    w_in_p = jnp.concatenate([w_in[:, :MLA_Q_RANK + MLA_KV_RANK], pe1, pe2, pe2, pe1], axis=1)
    wq = w_q_up.reshape(MLA_Q_RANK, MLA_HEADS, MLA_NOPE + MLA_ROPE)
    q1, q2 = wq[..., MLA_NOPE:MLA_NOPE + half], wq[..., MLA_NOPE + half:]
    wq_p = jnp.concatenate([wq[..., :MLA_NOPE], q1, q2, q2, q1], axis=-1).reshape(MLA_Q_RANK, MLA_HEADS * MLA_QK_PAD)
    wkv = w_kv_up.reshape(MLA_KV_RANK, MLA_HEADS, MLA_NOPE + MLA_V)
    wk = wkv[..., :MLA_NOPE].reshape(MLA_KV_RANK, MLA_HEADS * MLA_NOPE)
    wv = wkv[..., MLA_NOPE:].reshape(MLA_KV_RANK, MLA_HEADS * MLA_V)
    return w_in_p.astype(BF16), wq_p.astype(BF16), wk.astype(BF16), wv.astype(BF16)


def _mla_in(x, mod, ng, rope_a, rope_b, w_in, q_norm, kv_norm, w_q_up, w_kv_up):
    s, d = x.shape
    tm = ROW_TILE
    w_in_p, wq_p, wk, wv = _mla_weights(w_in, w_q_up, w_kv_up)
    q_scale = (MLA_NOPE + MLA_ROPE) ** -0.5 * LOG2E
    row = lambda n: pl.BlockSpec((tm, n), lambda i: (i, 0))
    hq, hv = MLA_HEADS * MLA_QK_PAD, MLA_HEADS * MLA_V
    return pl.pallas_call(
        functools.partial(_mla_in_kernel, q_scale=q_scale),
        out_shape=[jax.ShapeDtypeStruct((s, hq), BF16), jax.ShapeDtypeStruct((s, hq), BF16),
                   jax.ShapeDtypeStruct((s, hv), BF16)],
        grid=(s // tm,),
        in_specs=[row(d), _full(mod.shape), _full(ng.shape), row(LANES), row(LANES), _full(w_in_p.shape),
                  _full((1, MLA_Q_RANK)), _full((1, MLA_KV_RANK)), _full(wq_p.shape), _full(wk.shape),
                  _full(wv.shape)],
        out_specs=[row(hq), row(hq), row(hv)],
        compiler_params=_params("arbitrary"),
        name="mla_in",
    )(x, mod, ng, rope_a, rope_b, w_in_p, q_norm.reshape(1, -1), kv_norm.reshape(1, -1), wq_p, wk, wv)


def _mla_attn_kernel(q_ref, k_ref, v_ref, o_ref, m_ref, l_ref, acc_ref, *, blk):
    i = pl.program_id(1)
    q = q_ref[...]
    m_ref[...] = jnp.full(m_ref.shape, MASK_VALUE, F32)
    l_ref[...] = jnp.zeros(l_ref.shape, F32)
    acc_ref[...] = jnp.zeros(acc_ref.shape, F32)

    def step(j, masked):
        start = pl.multiple_of(j * blk, blk)
        s = _dot_nt(q, k_ref[pl.ds(start, blk), :])
        if masked:
            row = lax.broadcasted_iota(jnp.int32, (blk, blk), 0)
            col = lax.broadcasted_iota(jnp.int32, (blk, blk), 1)
            s = jnp.where(col <= row, s, MASK_VALUE)
        m_prev = m_ref[...]
        m_new = jnp.maximum(m_prev, jnp.max(s, axis=1, keepdims=True))
        alpha = jnp.exp2(m_prev - m_new)
        p = jnp.exp2(s - _tile_lanes(m_new, blk))
        l_ref[...] = alpha * l_ref[...] + jnp.sum(p, axis=1, keepdims=True)
        acc_ref[...] = alpha * acc_ref[...] + _dot(p.astype(BF16), v_ref[pl.ds(start, blk), :])
        m_ref[...] = m_new

    def body(j, carry):
        step(j, False)
        return carry

    lax.fori_loop(0, i, body, 0)
    step(i, True)
    o_ref[...] = (acc_ref[...] / l_ref[...]).astype(o_ref.dtype)


def _mla_attn(q, k, v):
    s = q.shape[0]
    blk = MLA_BLOCK
    return pl.pallas_call(
        functools.partial(_mla_attn_kernel, blk=blk),
        out_shape=jax.ShapeDtypeStruct((s, MLA_HEADS * MLA_V), BF16),
        grid=(MLA_HEADS, s // blk),
        in_specs=[pl.BlockSpec((blk, MLA_QK_PAD), lambda h, i: (i, h)),
                  pl.BlockSpec((s, MLA_QK_PAD), lambda h, i: (0, h)),
                  pl.BlockSpec((s, MLA_V), lambda h, i: (0, h))],
        out_specs=pl.BlockSpec((blk, MLA_V), lambda h, i: (i, h)),
        scratch_shapes=[pltpu.VMEM((blk, LANES), F32), pltpu.VMEM((blk, LANES), F32),
                        pltpu.VMEM((blk, MLA_V), F32)],
        compiler_params=_params("arbitrary", "arbitrary"),
        name="mla_attn",
    )(q, k, v)


def _sb_in_kernel(x_ref, mod_ref, ng_ref, w_ref, q_ref, k_ref, v_ref, *, q_scale):
    h = _modulated(x_ref[...], ng_ref[0:1, :], mod_ref[0:1, :], mod_ref[1:2, :])
    qkv = _dot(h, w_ref[...])
    n = SB_HEADS * SB_HEAD_DIM
    q_ref[...] = (qkv[:, :n] * q_scale).astype(BF16)
    k_ref[...] = qkv[:, n:2 * n].astype(BF16)
    v = qkv[:, 2 * n:]
    lane = lax.broadcasted_iota(jnp.int32, (v.shape[0], LANES), 1)
    low = lane < SB_HEAD_DIM
    parts = []
    for pair in range(SB_HEADS // 2):
        vp = v[:, pair * LANES:(pair + 1) * LANES]
        parts.append(jnp.where(low, vp, 0.0).astype(BF16))
        parts.append(jnp.where(low, 0.0, vp).astype(BF16))
    v_ref[...] = jnp.concatenate(parts, axis=1)


def _sb_in(x, mod, ng, w_in):
    s, d = x.shape
    tm = ROW_TILE
    n = SB_HEADS * SB_HEAD_DIM
    row = lambda w: pl.BlockSpec((tm, w), lambda i: (i, 0))
    w = w_in.astype(BF16)
    return pl.pallas_call(
        functools.partial(_sb_in_kernel, q_scale=SB_HEAD_DIM ** -0.5),
        out_shape=[jax.ShapeDtypeStruct((s, n), BF16), jax.ShapeDtypeStruct((s, n), BF16),
                   jax.ShapeDtypeStruct((s, 2 * n), BF16)],
        grid=(s // tm,),
        in_specs=[row(d), _full(mod.shape), _full(ng.shape), _full(w.shape)],
        out_specs=[row(n), row(n), row(2 * n)],
        compiler_params=_params("arbitrary"),
        name="sb_in",
    )(x, mod, ng, w)


def _sb_attn_kernel(q_ref, k_ref, v_ref, tri_ref, o_ref, acc_ref, c_ref, *, blk):
    i = pl.program_id(1)
    q = q_ref[...]
    lane = lax.broadcasted_iota(jnp.int32, q.shape, 1)
    zero = jnp.zeros_like(q)
    q_heads = (jnp.where(lane < SB_HEAD_DIM, q, zero), jnp.where(lane < SB_HEAD_DIM, zero, q))
    acc_ref[...] = jnp.zeros(acc_ref.shape, F32)
    c_ref[...] = jnp.zeros(c_ref.shape, F32)
    tri = tri_ref[...]

    def step(j, masked):
        start = pl.multiple_of(j * blk, blk)
        k = k_ref[pl.ds(start, blk), :]
        v = v_ref[pl.ds(start, blk), :]
        if masked:
            row = lax.broadcasted_iota(jnp.int32, (blk, blk), 0)
            col = lax.broadcasted_iota(jnp.int32, (blk, blk), 1)
            strict = col < row
        upd = None
        for hd in range(2):
            z = _dot_nt(q_heads[hd], k)
            sp = jnp.maximum(z, 0.0) + jnp.log(1.0 + jnp.exp(-jnp.abs(z)))
            log_1m = -sp
            log_b = z - sp
            if masked:
                log_1m = jnp.where(strict, log_1m, 0.0)
            hi = log_1m.astype(BF16)
            lo = (log_1m - hi.astype(F32)).astype(BF16)
            suffix = _dot(hi, tri) + _dot(lo, tri)
            c = c_ref[hd]
            a = jnp.exp(log_b + suffix + _tile_lanes(c, blk))
            if masked:
                a = jnp.where(strict, a, 0.0)
            pv = _dot(a.astype(BF16), v[:, hd * LANES:(hd + 1) * LANES])
            upd = pv if upd is None else upd + pv
            c_ref[hd] = c + jnp.sum(log_1m, axis=1, keepdims=True)
        acc_ref[...] += upd

    step(i, True)

    def body(t, carry):
        step(i - 1 - t, False)
        return carry

    lax.fori_loop(0, i, body, 0)
    o_ref[...] = acc_ref[...].astype(o_ref.dtype)


def _sb_attn(q, k, v):
    s = q.shape[0]
    blk = SB_BLOCK
    tri = jnp.tril(jnp.ones((blk, blk), F32), k=-1).astype(BF16)
    return pl.pallas_call(
        functools.partial(_sb_attn_kernel, blk=blk),
        out_shape=jax.ShapeDtypeStruct((s, SB_HEADS * SB_HEAD_DIM), BF16),
        grid=(SB_HEADS // 2, s // blk),
        in_specs=[pl.BlockSpec((blk, LANES), lambda p, i: (i, p)),
                  pl.BlockSpec((s, LANES), lambda p, i: (0, p)),
                  pl.BlockSpec((s, 2 * LANES), lambda p, i: (0, p)),
                  _full((blk, blk))],
        out_specs=pl.BlockSpec((blk, LANES), lambda p, i: (i, p)),
        scratch_shapes=[pltpu.VMEM((blk, LANES), F32), pltpu.VMEM((2, blk, LANES), F32)],
        compiler_params=_params("arbitrary", "arbitrary"),
        name="sb_attn",
    )(q, k, v, tri)


def _ret_in_kernel(x_ref, mod_ref, ng_ref, cos_ref, sin_ref, w_ref, q_ref, k_ref, v_ref, g_ref, *, k_scale):
    h = _modulated(x_ref[...], ng_ref[0:1, :], mod_ref[0:1, :], mod_ref[1:2, :])
    y = _dot(h, w_ref[...])
    hk, hv = RET_HEADS * RET_KEY_DIM, RET_HEADS * RET_VAL_DIM
    cos, sin = cos_ref[...], sin_ref[...]
    half = RET_KEY_DIM // 2

    def rope(t, scale):
        parts = []
        for hd in range(RET_HEADS):
            t1 = t[:, hd * RET_KEY_DIM:hd * RET_KEY_DIM + half]
            t2 = t[:, hd * RET_KEY_DIM + half:(hd + 1) * RET_KEY_DIM]
            parts.append(((t1 * cos - t2 * sin) * scale).astype(BF16))
            parts.append(((t2 * cos + t1 * sin) * scale).astype(BF16))
        return jnp.concatenate(parts, axis=1)

    q_ref[...] = rope(y[:, :hk], 1.0)
    k_ref[...] = rope(y[:, hk:2 * hk], k_scale)
    v_ref[...] = y[:, 2 * hk:2 * hk + hv].astype(BF16)
    g_ref[...] = y[:, 2 * hk + hv:]


def _ret_in(x, mod, ng, cos, sin, w_in):
    s, d = x.shape
    tm = ROW_TILE
    hk, hv = RET_HEADS * RET_KEY_DIM, RET_HEADS * RET_VAL_DIM
    row = lambda w: pl.BlockSpec((tm, w), lambda i: (i, 0))
    w = w_in.astype(BF16)
    return pl.pallas_call(
        functools.partial(_ret_in_kernel, k_scale=RET_KEY_DIM ** -0.5),
        out_shape=[jax.ShapeDtypeStruct((s, hk), BF16), jax.ShapeDtypeStruct((s, hk), BF16),
                   jax.ShapeDtypeStruct((s, hv), BF16), jax.ShapeDtypeStruct((s, hv), F32)],
        grid=(s // tm,),
        in_specs=[row(d), _full(mod.shape), _full(ng.shape), row(LANES), row(LANES), _full(w.shape)],
        out_specs=[row(hk), row(hk), row(hv), row(hv)],
        compiler_params=_params("arbitrary"),
        name="ret_in",
    )(x, mod, ng, cos, sin, w)


def _ret_core_kernel(lg_ref, q_ref, k_ref, v_ref, g_ref, gn_ref, o_ref, state_ref, *, chunk):
    @pl.when(pl.program_id(0) == 0)
    def _():
        state_ref[...] = jnp.zeros(state_ref.shape, F32)

    row = lax.broadcasted_iota(jnp.int32, (chunk, chunk), 0)
    col = lax.broadcasted_iota(jnp.int32, (chunk, chunk), 1)
    diff = (row - col).astype(F32)
    idx = lax.broadcasted_iota(jnp.int32, (chunk, LANES), 0).astype(F32)
    for hd in range(RET_HEADS):
        lg = lg_ref[hd]
        lg_row = lg[0:1, :]
        decay = jnp.where(diff >= 0.0, jnp.exp(jnp.maximum(diff, 0.0) * _tile_lanes(lg_row, chunk)), 0.0)
        xi = jnp.exp((idx + 1.0) * lg_row)
        zeta = jnp.exp((chunk - 1.0 - idx) * lg_row)
        g_chunk = jnp.exp(chunk * lg)
        q = q_ref[:, hd * RET_KEY_DIM:(hd + 1) * RET_KEY_DIM]
        k = k_ref[:, hd * RET_KEY_DIM:(hd + 1) * RET_KEY_DIM]
        v = v_ref[:, hd * RET_VAL_DIM:(hd + 1) * RET_VAL_DIM]
        state = state_ref[hd]
        sc = _dot_nt(q, k) * decay
        inner = _dot(sc.astype(BF16), v)
        cross = _dot(q, state.astype(BF16)) * _tile_lanes(xi, RET_VAL_DIM)
        kz = (k.astype(F32) * _tile_lanes(zeta, RET_KEY_DIM)).astype(BF16)
        upd = lax.dot_general(kz, v, (((0,), (0,)), ((), ())), preferred_element_type=F32)
        state_ref[hd] = state * _tile_lanes(g_chunk[0:1, :], RET_VAL_DIM) + upd
        o = inner + cross
        o = o * lax.rsqrt(jnp.mean(o * o, axis=-1, keepdims=True) + EPS)
        o = o * gn_ref[:, hd * RET_VAL_DIM:(hd + 1) * RET_VAL_DIM]
        g = g_ref[:, hd * RET_VAL_DIM:(hd + 1) * RET_VAL_DIM]
        o_ref[:, hd * RET_VAL_DIM:(hd + 1) * RET_VAL_DIM] = (g * (1.0 / (1.0 + jnp.exp(-g))) * o).astype(o_ref.dtype)


def _ret_core(q, k, v, g, gn_g):
    s = q.shape[0]
    chunk = RET_CHUNK
    hk, hv = RET_HEADS * RET_KEY_DIM, RET_HEADS * RET_VAL_DIM
    log_gamma = jnp.log(1.0 - 2.0 ** (-5.0 - jnp.arange(RET_HEADS, dtype=F32)))
    lg = jnp.broadcast_to(log_gamma[:, None, None], (RET_HEADS, SUBLANES, LANES))
    row = lambda w: pl.BlockSpec((chunk, w), lambda i: (i, 0))
    return pl.pallas_call(
        functools.partial(_ret_core_kernel, chunk=chunk),
        out_shape=jax.ShapeDtypeStruct((s, hv), BF16),
        grid=(s // chunk,),
        in_specs=[_full(lg.shape), row(hk), row(hk), row(hv), row(hv), _full((1, hv))],
        out_specs=row(hv),
        scratch_shapes=[pltpu.VMEM((RET_HEADS, RET_KEY_DIM, RET_VAL_DIM), F32)],
        compiler_params=_params("arbitrary"),
        name="ret_core",
    )(lg, q, k, v, g, gn_g.reshape(1, hv))


def _post_kernel(o_ref, x_ref, mod_ref, ng_ref, wo_ref, w1_ref, w2_ref, out_ref, y_ref):
    x = x_ref[...]
    y = _dot(o_ref[...], wo_ref[...])
    x = x + mod_ref[2:3, :] * _rms(y, ng_ref[1:2, :])
    h = _modulated(x, ng_ref[2:3, :], mod_ref[3:4, :], mod_ref[4:5, :])
    d_ff = w1_ref.shape[1]
    for c in range(d_ff // FF_CHUNK):
        a = jnp.maximum(_dot(h, w1_ref[:, c * FF_CHUNK:(c + 1) * FF_CHUNK]), 0.0)
        part = _dot((a * a).astype(BF16), w2_ref[c * FF_CHUNK:(c + 1) * FF_CHUNK, :])
        if c == 0:
            y_ref[...] = part
        else:
            y_ref[...] += part
    out_ref[...] = x + mod_ref[5:6, :] * _rms(y_ref[...], ng_ref[3:4, :])


def _post(o, x, mod, ng, w_out, w1, w2):
    s, d = x.shape
    tm = ROW_TILE
    ko = o.shape[1]
    row = lambda w: pl.BlockSpec((tm, w), lambda i: (i, 0))
    resident = lambda a: pl.BlockSpec(a.shape, lambda i: (0, 0), pipeline_mode=pl.Buffered(1))
    wo, w1b, w2b = w_out.astype(BF16), w1.astype(BF16), w2.astype(BF16)
    return pl.pallas_call(
        _post_kernel,
        out_shape=jax.ShapeDtypeStruct((s, d), F32),
        grid=(s // tm,),
        in_specs=[row(ko), row(d), _full(mod.shape), _full(ng.shape), resident(wo), resident(w1b), resident(w2b)],
        out_specs=row(d),
        scratch_shapes=[pltpu.VMEM((tm, d), F32)],
        compiler_params=_params("arbitrary"),
        name="post",
    )(o, x, mod, ng, wo, w1b, w2b)


def kernel(x, c, positions, ada_w, ada_b, norm_g, ffn_w1, ffn_w2, mla_w_in, mla_q_norm, mla_kv_norm, mla_w_q_up,
           mla_w_kv_up, mla_w_out, sb_w_in, sb_w_out, ret_w_in, ret_gn_g, ret_w_out):
    batch, seq, d = x.shape
    assert batch == 1 and seq % MLA_BLOCK == 0 and seq % ROW_TILE == 0
    depth = ada_w.shape[0]
    mod = _adaln(c, ada_w, ada_b)
    rope_a, rope_b, cos, sin = _rope_tables(positions)
    xs = x.reshape(seq, d)
    for i in range(depth):
        kind, j = i % 3, i // 3
        if kind == 0:
            q, k, v = _mla_in(xs, mod[i], norm_g[i], rope_a, rope_b, mla_w_in[j], mla_q_norm[j], mla_kv_norm[j],
                              mla_w_q_up[j], mla_w_kv_up[j])
            o, w_out = _mla_attn(q, k, v), mla_w_out[j]
        elif kind == 1:
            q, k, v = _sb_in(xs, mod[i], norm_g[i], sb_w_in[j])
            o, w_out = _sb_attn(q, k, v), sb_w_out[j]
        else:
            q, k, v, g = _ret_in(xs, mod[i], norm_g[i], cos, sin, ret_w_in[j])
            o, w_out = _ret_core(q, k, v, g, ret_gn_g[j]), ret_w_out[j]
        xs = _post(o, xs, mod[i], norm_g[i], w_out, ffn_w1[i], ffn_w2[i])
    return xs.reshape(batch, seq, d)
```

```python
import functools
import math

import jax
import jax.numpy as jnp
from jax import lax
from jax.experimental import pallas as pl
from jax.experimental.pallas import tpu as pltpu

EPS = 1e-6
ROPE_BASE = 10000.0
MASK_VALUE = -1e30
LOG2E = math.log2(math.e)

LANES = 128
SUBLANES = 8
VMEM_LIMIT_BYTES = 60000 * 1024

MLA_HEADS = 8
MLA_Q_RANK = 256
MLA_KV_RANK = 128
MLA_NOPE = 128
MLA_ROPE = 64
MLA_V = 128
MLA_QK_PAD = 256

SB_HEADS = 16
SB_HEAD_DIM = 64

RET_HEADS = 4
RET_KEY_DIM = 256
RET_VAL_DIM = 512

ROW_TILE = 512
FF_CHUNK = 1024
MLA_BLOCK = 512
SB_Q_BLOCK = 512
SB_K_BLOCK = 256
SB_HEADS_PER_STEP = 4
SB_EXP2_CLAMP = 126.0
RET_CHUNK = 256

F32 = jnp.float32
BF16 = jnp.bfloat16


def _params(*semantics):
    return pltpu.CompilerParams(dimension_semantics=semantics, vmem_limit_bytes=VMEM_LIMIT_BYTES)


def _full(shape):
    nd = len(shape)
    return pl.BlockSpec(shape, lambda *_: (0,) * nd)


def _rms(x, g):
    return x * lax.rsqrt(jnp.mean(x * x, axis=-1, keepdims=True) + EPS) * g


def _dot(a, b):
    return jnp.dot(a, b, preferred_element_type=F32)


def _dot_nt(a, b):
    return lax.dot_general(a, b, (((1,), (1,)), ((), ())), preferred_element_type=F32)


def _tile_lanes(v, width):
    reps = width // LANES
    return v if reps == 1 else jnp.concatenate([v] * reps, axis=1)


def _adaln_kernel(c_ref, w_ref, b_ref, o_ref):
    c = c_ref[...]
    cond = c * (1.0 / (1.0 + jnp.exp(-c)))
    o_ref[0] = jnp.sum(w_ref[0] * cond, axis=0, keepdims=True) + b_ref[0]


def _adaln(c, ada_w, ada_b):
    depth, d, n = ada_w.shape
    tn = 768
    out = pl.pallas_call(
        _adaln_kernel,
        out_shape=jax.ShapeDtypeStruct((depth, 1, n), F32),
        grid=(depth, n // tn),
        in_specs=[
            _full((d, 1)),
            pl.BlockSpec((1, d, tn), lambda i, j: (i, 0, j)),
            pl.BlockSpec((1, 1, tn), lambda i, j: (i, 0, j)),
        ],
        out_specs=pl.BlockSpec((1, 1, tn), lambda i, j: (i, 0, j)),
        compiler_params=_params("arbitrary", "arbitrary"),
        name="adaln",
    )(c.reshape(d, 1), ada_w, ada_b.reshape(depth, 1, n))
    return out.reshape(depth, 6, d)


def _rope_kernel(pos_ref, f64_ref, f256_ref, a_mask_ref, b_sign_ref, a_ref, b_ref, cos_ref, sin_ref):
    pos = pos_ref[...]
    ang = pos * f64_ref[...]
    a_ref[...] = jnp.cos(ang) * a_mask_ref[...]
    b_ref[...] = jnp.sin(ang) * b_sign_ref[...]
    ang = pos * f256_ref[...]
    cos_ref[...] = jnp.cos(ang)
    sin_ref[...] = jnp.sin(ang)


def _rope_tables(positions):
    s = positions.shape[-1]
    tm = 1024
    pos = positions.astype(F32).reshape(s, 1)
    f64 = ROPE_BASE ** (-jnp.arange(0, MLA_ROPE, 2, dtype=F32) / MLA_ROPE)
    f256 = ROPE_BASE ** (-jnp.arange(0, RET_KEY_DIM, 2, dtype=F32) / RET_KEY_DIM)
    quarter = MLA_ROPE // 2
    ones, zeros = jnp.ones((quarter,), F32), jnp.zeros((quarter,), F32)
    a_mask = jnp.concatenate([ones, ones, zeros, zeros]).reshape(1, LANES)
    b_sign = jnp.concatenate([-ones, ones, zeros, zeros]).reshape(1, LANES)
    row = pl.BlockSpec((tm, LANES), lambda i: (i, 0))
    return pl.pallas_call(
        _rope_kernel,
        out_shape=[jax.ShapeDtypeStruct((s, LANES), F32)] * 4,
        grid=(s // tm,),
        in_specs=[pl.BlockSpec((tm, 1), lambda i: (i, 0))] + [_full((1, LANES))] * 4,
        out_specs=[row] * 4,
        compiler_params=_params("arbitrary"),
        name="rope_tables",
    )(pos, jnp.tile(f64, 4).reshape(1, LANES), f256.reshape(1, LANES), a_mask, b_sign)


def _modulated(x, g, shift, scale):
    return (_rms(x, g) * (1.0 + scale) + shift).astype(BF16)


def _mla_in_kernel(x_ref, mod_ref, ng_ref, a_ref, b_ref, w_in_ref, qn_ref, kvn_ref, wq_ref, wk_ref, wv_ref,
                   q_ref, k_ref, v_ref, *, q_scale):
    h = _modulated(x_ref[...], ng_ref[0:1, :], mod_ref[0:1, :], mod_ref[1:2, :])
    lat = _dot(h, w_in_ref[...])
    rope_a, rope_b = a_ref[...], b_ref[...]

    def rope(v):
        return v * rope_a + pltpu.roll(v, 2 * (MLA_ROPE // 2), axis=1) * rope_b

    q_lat = lat[:, :MLA_Q_RANK]
    kv_lat = lat[:, MLA_Q_RANK:MLA_Q_RANK + MLA_KV_RANK]
    k_pe = rope(lat[:, MLA_Q_RANK + MLA_KV_RANK:]).astype(BF16)
    q = _dot(_rms(q_lat, qn_ref[...]).astype(BF16), wq_ref[...])
    kvn = _rms(kv_lat, kvn_ref[...]).astype(BF16)
    k_nope = _dot(kvn, wk_ref[...]).astype(BF16)
    v_ref[...] = _dot(kvn, wv_ref[...]).astype(BF16)
    q_parts, k_parts = [], []
    for hd in range(MLA_HEADS):
        lo = hd * MLA_QK_PAD
        q_parts.append((q[:, lo:lo + MLA_NOPE] * q_scale).astype(BF16))
        q_parts.append((rope(q[:, lo + MLA_NOPE:lo + MLA_QK_PAD]) * q_scale).astype(BF16))
        k_parts.append(k_nope[:, hd * MLA_NOPE:(hd + 1) * MLA_NOPE])
        k_parts.append(k_pe)
    q_ref[...] = jnp.concatenate(q_parts, axis=1)
    k_ref[...] = jnp.concatenate(k_parts, axis=1)


def _mla_weights(w_in, w_q_up, w_kv_up):
    half = MLA_ROPE // 2
    pe = w_in[:, MLA_Q_RANK + MLA_KV_RANK:]
    pe1, pe2 = pe[:, :half], pe[:, half:]
    w_in_p = jnp.concatenate([w_in[:, :MLA_Q_RANK + MLA_KV_RANK], pe1, pe2, pe2, pe1], axis=1)
    wq = w_q_up.reshape(MLA_Q_RANK, MLA_HEADS, MLA_NOPE + MLA_ROPE)
    q1, q2 = wq[..., MLA_NOPE:MLA_NOPE + half], wq[..., MLA_NOPE + half:]
    wq_p = jnp.concatenate([wq[..., :MLA_NOPE], q1, q2, q2, q1], axis=-1).reshape(MLA_Q_RANK, MLA_HEADS * MLA_QK_PAD)
    wkv = w_kv_up.reshape(MLA_KV_RANK, MLA_HEADS, MLA_NOPE + MLA_V)
    wk = wkv[..., :MLA_NOPE].reshape(MLA_KV_RANK, MLA_HEADS * MLA_NOPE)
    wv = wkv[..., MLA_NOPE:].reshape(MLA_KV_RANK, MLA_HEADS * MLA_V)
    return w_in_p.astype(BF16), wq_p.astype(BF16), wk.astype(BF16), wv.astype(BF16)


def _mla_in(x, mod, ng, rope_a, rope_b, w_in, q_norm, kv_norm, w_q_up, w_kv_up):
    s, d = x.shape
    tm = ROW_TILE
    w_in_p, wq_p, wk, wv = _mla_weights(w_in, w_q_up, w_kv_up)
    q_scale = (MLA_NOPE + MLA_ROPE) ** -0.5 * LOG2E
    row = lambda n: pl.BlockSpec((tm, n), lambda i: (i, 0))
    hq, hv = MLA_HEADS * MLA_QK_PAD, MLA_HEADS * MLA_V
    return pl.pallas_call(
        functools.partial(_mla_in_kernel, q_scale=q_scale),
        out_shape=[jax.ShapeDtypeStruct((s, hq), BF16), jax.ShapeDtypeStruct((s, hq), BF16),
                   jax.ShapeDtypeStruct((s, hv), BF16)],
        grid=(s // tm,),
        in_specs=[row(d), _full(mod.shape), _full(ng.shape), row(LANES), row(LANES), _full(w_in_p.shape),
                  _full((1, MLA_Q_RANK)), _full((1, MLA_KV_RANK)), _full(wq_p.shape), _full(wk.shape),
                  _full(wv.shape)],
        out_specs=[row(hq), row(hq), row(hv)],
        compiler_params=_params("arbitrary"),
        name="mla_in",
    )(x, mod, ng, rope_a, rope_b, w_in_p, q_norm.reshape(1, -1), kv_norm.reshape(1, -1), wq_p, wk, wv)


def _mla_attn_kernel(q_ref, k_ref, v_ref, o_ref, m_ref, l_ref, acc_ref, *, blk):
    i = pl.program_id(1)
    q = q_ref[...]
    m_ref[...] = jnp.full(m_ref.shape, MASK_VALUE, F32)
    l_ref[...] = jnp.zeros(l_ref.shape, F32)
    acc_ref[...] = jnp.zeros(acc_ref.shape, F32)

    def step(j, masked):
        start = pl.multiple_of(j * blk, blk)
        s = _dot_nt(q, k_ref[pl.ds(start, blk), :])
        if masked:
            row = lax.broadcasted_iota(jnp.int32, (blk, blk), 0)
            col = lax.broadcasted_iota(jnp.int32, (blk, blk), 1)
            s = jnp.where(col <= row, s, MASK_VALUE)
        m_prev = m_ref[...]
        m_new = jnp.maximum(m_prev, jnp.max(s, axis=1, keepdims=True))
        alpha = jnp.exp2(m_prev - m_new)
        p = jnp.exp2(s - _tile_lanes(m_new, blk))
        l_ref[...] = alpha * l_ref[...] + jnp.sum(p, axis=1, keepdims=True)
        acc_ref[...] = alpha * acc_ref[...] + _dot(p.astype(BF16), v_ref[pl.ds(start, blk), :])
        m_ref[...] = m_new

    def body(j, carry):
        step(j, False)
        return carry

    lax.fori_loop(0, i, body, 0)
    step(i, True)
    o_ref[...] = (acc_ref[...] / l_ref[...]).astype(o_ref.dtype)


def _mla_attn(q, k, v):
    s = q.shape[0]
    blk = MLA_BLOCK
    return pl.pallas_call(
        functools.partial(_mla_attn_kernel, blk=blk),
        out_shape=jax.ShapeDtypeStruct((s, MLA_HEADS * MLA_V), BF16),
        grid=(MLA_HEADS, s // blk),
        in_specs=[pl.BlockSpec((blk, MLA_QK_PAD), lambda h, i: (i, h)),
                  pl.BlockSpec((s, MLA_QK_PAD), lambda h, i: (0, h)),
                  pl.BlockSpec((s, MLA_V), lambda h, i: (0, h))],
        out_specs=pl.BlockSpec((blk, MLA_V), lambda h, i: (i, h)),
        scratch_shapes=[pltpu.VMEM((blk, LANES), F32), pltpu.VMEM((blk, LANES), F32),
                        pltpu.VMEM((blk, MLA_V), F32)],
        compiler_params=_params("arbitrary", "arbitrary"),
        name="mla_attn",
    )(q, k, v)


def _mla_mixer(x, mod, ng, tables, w_in, q_norm, kv_norm, w_q_up, w_kv_up):
    q, k, v = _mla_in(x, mod, ng, tables[0], tables[1], w_in, q_norm, kv_norm, w_q_up, w_kv_up)
    return _mla_attn(q, k, v)


def _sb_in_kernel(x_ref, mod_ref, ng_ref, w_ref, q_ref, k_ref, v_ref, *, q_scale):
    h = _modulated(x_ref[...], ng_ref[0:1, :], mod_ref[0:1, :], mod_ref[1:2, :])
    qkv = _dot(h, w_ref[...])
    n = SB_HEADS * SB_HEAD_DIM
    q_ref[...] = (qkv[:, :n] * q_scale).astype(BF16)
    k_ref[...] = qkv[:, n:2 * n].astype(BF16)
    v_ref[...] = qkv[:, 2 * n:].astype(BF16)


def _sb_in(x, mod, ng, w_in):
    s, d = x.shape
    tm = ROW_TILE
    n = SB_HEADS * SB_HEAD_DIM
    row = lambda w: pl.BlockSpec((tm, w), lambda i: (i, 0))
    w = w_in.astype(BF16)
    return pl.pallas_call(
        functools.partial(_sb_in_kernel, q_scale=-LOG2E * SB_HEAD_DIM ** -0.5),
        out_shape=[jax.ShapeDtypeStruct((s, n), BF16)] * 3,
        grid=(s // tm,),
        in_specs=[row(d), _full(mod.shape), _full(ng.shape), _full(w.shape)],
        out_specs=[row(n)] * 3,
        compiler_params=_params("arbitrary"),
        name="sb_in",
    )(x, mod, ng, w)


def _sb_attn_kernel(q_ref, k_ref, v_ref, tri_ref, o_ref, acc_ref, c_ref, *, bq, bk, heads):
    i = pl.program_id(1)
    width = heads * SB_HEAD_DIM
    q = q_ref[...]
    lane_q = lax.broadcasted_iota(jnp.int32, q.shape, 1)
    lane_v = lax.broadcasted_iota(jnp.int32, (bk, width), 1)

    def head_lanes(lane, hd):
        return (lane >= hd * SB_HEAD_DIM) & (lane < (hd + 1) * SB_HEAD_DIM)

    q_heads = [jnp.where(head_lanes(lane_q, hd), q, jnp.zeros_like(q)) for hd in range(heads)]
    acc_ref[...] = jnp.zeros(acc_ref.shape, F32)
    c_ref[...] = jnp.zeros(c_ref.shape, F32)
    tri = tri_ref[...]

    def step(j, mask_offset):
        start = pl.multiple_of(j * bk, bk)
        k = k_ref[pl.ds(start, bk), :]
        v = v_ref[pl.ds(start, bk), :]
        if mask_offset is not None:
            row = lax.broadcasted_iota(jnp.int32, (bq, bk), 0)
            col = lax.broadcasted_iota(jnp.int32, (bq, bk), 1)
            strict = (col - row) < mask_offset
        upd = None
        for hd in range(heads):
            w = _dot_nt(q_heads[hd], k)
            neg_log_b = jnp.log2(1.0 + jnp.exp2(jnp.minimum(w, SB_EXP2_CLAMP)))
            log_1m = jnp.minimum(w - neg_log_b, 0.0)
            if mask_offset is not None:
                log_1m = jnp.where(strict, log_1m, 0.0)
            suffix = _dot(log_1m.astype(BF16), tri)
            c = c_ref[hd]
            a = jnp.exp2(suffix - neg_log_b + _tile_lanes(c, bk))
            if mask_offset is not None:
                a = jnp.where(strict, a, 0.0)
            vh = jnp.where(head_lanes(lane_v, hd), v, jnp.zeros_like(v))
            pv = _dot(a.astype(BF16), vh)
            upd = pv if upd is None else upd + pv
            c_ref[hd] = c + jnp.sum(log_1m, axis=1, keepdims=True)
        acc_ref[...] += upd

    ratio = bq // bk
    for dj in reversed(range(ratio)):
        step(i * ratio + dj, -dj * bk)

    def body(t, carry):
        step(i * ratio - 1 - t, None)
        return carry

    lax.fori_loop(0, i * ratio, body, 0)
    o_ref[...] = acc_ref[...].astype(o_ref.dtype)


def _sb_attn(q, k, v):
    s = q.shape[0]
    bq, bk, heads = SB_Q_BLOCK, SB_K_BLOCK, SB_HEADS_PER_STEP
    width = heads * SB_HEAD_DIM
    tri = jnp.tril(jnp.ones((bk, bk), F32), k=-1).astype(BF16)
    resident = lambda: pl.BlockSpec((s, width), lambda p, i: (0, p), pipeline_mode=pl.Buffered(1))
    return pl.pallas_call(
        functools.partial(_sb_attn_kernel, bq=bq, bk=bk, heads=heads),
        out_shape=jax.ShapeDtypeStruct((s, SB_HEADS * SB_HEAD_DIM), BF16),
        grid=(SB_HEADS // heads, s // bq),
        in_specs=[pl.BlockSpec((bq, width), lambda p, i: (i, p)), resident(), resident(), _full((bk, bk))],
        out_specs=pl.BlockSpec((bq, width), lambda p, i: (i, p)),
        scratch_shapes=[pltpu.VMEM((bq, width), F32), pltpu.VMEM((heads, bq, LANES), F32)],
        compiler_params=_params("arbitrary", "arbitrary"),
        name="sb_attn",
    )(q, k, v, tri)


def _ret_in_kernel(x_ref, mod_ref, ng_ref, cos_ref, sin_ref, w_ref, q_ref, k_ref, v_ref, g_ref, *, k_scale):
    h = _modulated(x_ref[...], ng_ref[0:1, :], mod_ref[0:1, :], mod_ref[1:2, :])
    y = _dot(h, w_ref[...])
    hk, hv = RET_HEADS * RET_KEY_DIM, RET_HEADS * RET_VAL_DIM
    cos, sin = cos_ref[...], sin_ref[...]
    half = RET_KEY_DIM // 2

    def rope(t, scale):
        parts = []
        for hd in range(RET_HEADS):
            t1 = t[:, hd * RET_KEY_DIM:hd * RET_KEY_DIM + half]
            t2 = t[:, hd * RET_KEY_DIM + half:(hd + 1) * RET_KEY_DIM]
            parts.append(((t1 * cos - t2 * sin) * scale).astype(BF16))
            parts.append(((t2 * cos + t1 * sin) * scale).astype(BF16))
        return jnp.concatenate(parts, axis=1)

    q_ref[...] = rope(y[:, :hk], 1.0)
    k_ref[...] = rope(y[:, hk:2 * hk], k_scale)
    v_ref[...] = y[:, 2 * hk:2 * hk + hv].astype(BF16)
    g_ref[...] = y[:, 2 * hk + hv:]


def _ret_in(x, mod, ng, cos, sin, w_in):
    s, d = x.shape
    tm = ROW_TILE
    hk, hv = RET_HEADS * RET_KEY_DIM, RET_HEADS * RET_VAL_DIM
    row = lambda w: pl.BlockSpec((tm, w), lambda i: (i, 0))
    w = w_in.astype(BF16)
    return pl.pallas_call(
        functools.partial(_ret_in_kernel, k_scale=RET_KEY_DIM ** -0.5),
        out_shape=[jax.ShapeDtypeStruct((s, hk), BF16), jax.ShapeDtypeStruct((s, hk), BF16),
                   jax.ShapeDtypeStruct((s, hv), BF16), jax.ShapeDtypeStruct((s, hv), F32)],
        grid=(s // tm,),
        in_specs=[row(d), _full(mod.shape), _full(ng.shape), row(LANES), row(LANES), _full(w.shape)],
        out_specs=[row(hk), row(hk), row(hv), row(hv)],
        compiler_params=_params("arbitrary"),
        name="ret_in",
    )(x, mod, ng, cos, sin, w)


def _ret_core_kernel(lg_ref, q_ref, k_ref, v_ref, g_ref, gn_ref, o_ref, state_ref, *, chunk):
    @pl.when(pl.program_id(0) == 0)
    def _():
        state_ref[...] = jnp.zeros(state_ref.shape, F32)

    row = lax.broadcasted_iota(jnp.int32, (chunk, chunk), 0)
    col = lax.broadcasted_iota(jnp.int32, (chunk, chunk), 1)
    diff = (row - col).astype(F32)
    idx = lax.broadcasted_iota(jnp.int32, (chunk, LANES), 0).astype(F32)
    for hd in range(RET_HEADS):
        lg = lg_ref[hd]
        lg_row = lg[0:1, :]
        decay = jnp.where(diff >= 0.0, jnp.exp(jnp.maximum(diff, 0.0) * _tile_lanes(lg_row, chunk)), 0.0)
        xi = jnp.exp((idx + 1.0) * lg_row)
        zeta = jnp.exp((chunk - 1.0 - idx) * lg_row)
        g_chunk = jnp.exp(chunk * lg)
        q = q_ref[:, hd * RET_KEY_DIM:(hd + 1) * RET_KEY_DIM]
        k = k_ref[:, hd * RET_KEY_DIM:(hd + 1) * RET_KEY_DIM]
        v = v_ref[:, hd * RET_VAL_DIM:(hd + 1) * RET_VAL_DIM]
        state = state_ref[hd]
        sc = _dot_nt(q, k) * decay
        inner = _dot(sc.astype(BF16), v)
        cross = _dot(q, state.astype(BF16)) * _tile_lanes(xi, RET_VAL_DIM)
        kz = (k.astype(F32) * _tile_lanes(zeta, RET_KEY_DIM)).astype(BF16)
        upd = lax.dot_general(kz, v, (((0,), (0,)), ((), ())), preferred_element_type=F32)
        state_ref[hd] = state * _tile_lanes(g_chunk[0:1, :], RET_VAL_DIM) + upd
        o = inner + cross
        o = o * lax.rsqrt(jnp.mean(o * o, axis=-1, keepdims=True) + EPS)
        o = o * gn_ref[:, hd * RET_VAL_DIM:(hd + 1) * RET_VAL_DIM]
        g = g_ref[:, hd * RET_VAL_DIM:(hd + 1) * RET_VAL_DIM]
        o_ref[:, hd * RET_VAL_DIM:(hd + 1) * RET_VAL_DIM] = (g * (1.0 / (1.0 + jnp.exp(-g))) * o).astype(o_ref.dtype)


def _ret_core(q, k, v, g, gn_g):
    s = q.shape[0]
    chunk = RET_CHUNK
    hk, hv = RET_HEADS * RET_KEY_DIM, RET_HEADS * RET_VAL_DIM
    log_gamma = jnp.log(1.0 - 2.0 ** (-5.0 - jnp.arange(RET_HEADS, dtype=F32)))
    lg = jnp.broadcast_to(log_gamma[:, None, None], (RET_HEADS, SUBLANES, LANES))
    row = lambda w: pl.BlockSpec((chunk, w), lambda i: (i, 0))
    return pl.pallas_call(
        functools.partial(_ret_core_kernel, chunk=chunk),
        out_shape=jax.ShapeDtypeStruct((s, hv), BF16),
        grid=(s // chunk,),
        in_specs=[_full(lg.shape), row(hk), row(hk), row(hv), row(hv), _full((1, hv))],
        out_specs=row(hv),
        scratch_shapes=[pltpu.VMEM((RET_HEADS, RET_KEY_DIM, RET_VAL_DIM), F32)],
        compiler_params=_params("arbitrary"),
        name="ret_core",
    )(lg, q, k, v, g, gn_g.reshape(1, hv))


def _post_kernel(o_ref, x_ref, mod_ref, ng_ref, wo_ref, w1_ref, w2_ref, out_ref, y_ref):
    x = x_ref[...]
    y = _dot(o_ref[...], wo_ref[...])
    x = x + mod_ref[2:3, :] * _rms(y, ng_ref[1:2, :])
    h = _modulated(x, ng_ref[2:3, :], mod_ref[3:4, :], mod_ref[4:5, :])
    d_ff = w1_ref.shape[1]
    for c in range(d_ff // FF_CHUNK):
        a = jnp.maximum(_dot(h, w1_ref[:, c * FF_CHUNK:(c + 1) * FF_CHUNK]), 0.0)
        part = _dot((a * a).astype(BF16), w2_ref[c * FF_CHUNK:(c + 1) * FF_CHUNK, :])
        if c == 0:
            y_ref[...] = part
        else:
            y_ref[...] += part
    out_ref[...] = x + mod_ref[5:6, :] * _rms(y_ref[...], ng_ref[3:4, :])


def _post(o, x, mod, ng, w_out, w1, w2):
    s, d = x.shape
    tm = ROW_TILE
    ko = o.shape[1]
    row = lambda w: pl.BlockSpec((tm, w), lambda i: (i, 0))
    resident = lambda a: pl.BlockSpec(a.shape, lambda i: (0, 0), pipeline_mode=pl.Buffered(1))
    wo, w1b, w2b = w_out.astype(BF16), w1.astype(BF16), w2.astype(BF16)
    return pl.pallas_call(
        _post_kernel,
        out_shape=jax.ShapeDtypeStruct((s, d), F32),
        grid=(s // tm,),
        in_specs=[row(ko), row(d), _full(mod.shape), _full(ng.shape), resident(wo), resident(w1b), resident(w2b)],
        out_specs=row(d),
        scratch_shapes=[pltpu.VMEM((tm, d), F32)],
        compiler_params=_params("arbitrary"),
        name="post",
    )(o, x, mod, ng, wo, w1b, w2b)


def kernel(x, c, positions, ada_w, ada_b, norm_g, ffn_w1, ffn_w2, mla_w_in, mla_q_norm, mla_kv_norm, mla_w_q_up,
           mla_w_kv_up, mla_w_out, sb_w_in, sb_w_out, ret_w_in, ret_gn_g, ret_w_out):
    batch, seq, d = x.shape
    assert batch == 1 and seq % MLA_BLOCK == 0 and seq % ROW_TILE == 0
    depth = ada_w.shape[0]
    mod = _adaln(c, ada_w, ada_b)
    tables = _rope_tables(positions)
    xs = x.reshape(seq, d)
    for i in range(depth):
        kind, j = i % 3, i // 3
        if kind == 0:
            o = _mla_mixer(xs, mod[i], norm_g[i], tables, mla_w_in[j], mla_q_norm[j], mla_kv_norm[j],
                           mla_w_q_up[j], mla_w_kv_up[j])
            w_out = mla_w_out[j]
        elif kind == 1:
            q, k, v = _sb_in(xs, mod[i], norm_g[i], sb_w_in[j])
            o, w_out = _sb_attn(q, k, v), sb_w_out[j]
        else:
            q, k, v, g = _ret_in(xs, mod[i], norm_g[i], tables[2], tables[3], ret_w_in[j])
            o, w_out = _ret_core(q, k, v, g, ret_gn_g[j]), ret_w_out[j]
        xs = _post(o, xs, mod[i], norm_g[i], w_out, ffn_w1[i], ffn_w2[i])
    return xs.reshape(batch, seq, d)
```

```python
import functools
import math

import jax
import jax.numpy as jnp
from jax import lax
from jax.experimental import pallas as pl
from jax.experimental.pallas import tpu as pltpu

EPS = 1e-6
ROPE_BASE = 10000.0
MASK_VALUE = -1e30
LOG2E = math.log2(math.e)

LANES = 128
SUBLANES = 8
VMEM_LIMIT_BYTES = 60000 * 1024

MLA_HEADS = 8
MLA_Q_RANK = 256
MLA_KV_RANK = 128
MLA_NOPE = 128
MLA_ROPE = 64
MLA_V = 128
MLA_QK_PAD = 256

SB_HEADS = 16
SB_HEAD_DIM = 64

RET_HEADS = 4
RET_KEY_DIM = 256
RET_VAL_DIM = 512

ROW_TILE = 512
FF_CHUNK = 1024
MLA_BLOCK = 512
MLA_Q_BLOCK = 1024
MLA_Q_GROUP = 256
SB_Q_BLOCK = 512
SB_K_BLOCK = 256
SB_HEADS_PER_STEP = 4
SB_EXP2_CLAMP = 126.0
RET_CHUNK = 256

F32 = jnp.float32
BF16 = jnp.bfloat16


def _params(*semantics):
    return pltpu.CompilerParams(dimension_semantics=semantics, vmem_limit_bytes=VMEM_LIMIT_BYTES)


def _full(shape):
    nd = len(shape)
    return pl.BlockSpec(shape, lambda *_: (0,) * nd)


def _rms(x, g):
    return x * lax.rsqrt(jnp.mean(x * x, axis=-1, keepdims=True) + EPS) * g


def _dot(a, b):
    return jnp.dot(a, b, preferred_element_type=F32)


def _dot_nt(a, b):
    return lax.dot_general(a, b, (((1,), (1,)), ((), ())), preferred_element_type=F32)


def _tile_lanes(v, width):
    reps = width // LANES
    return v if reps == 1 else jnp.concatenate([v] * reps, axis=1)


def _adaln_kernel(c_ref, w_ref, b_ref, o_ref):
    c = c_ref[...]
    cond = c * (1.0 / (1.0 + jnp.exp(-c)))
    o_ref[0] = jnp.sum(w_ref[0] * cond, axis=0, keepdims=True) + b_ref[0]


def _adaln(c, ada_w, ada_b):
    depth, d, n = ada_w.shape
    tn = 768
    out = pl.pallas_call(
        _adaln_kernel,
        out_shape=jax.ShapeDtypeStruct((depth, 1, n), F32),
        grid=(depth, n // tn),
        in_specs=[
            _full((d, 1)),
            pl.BlockSpec((1, d, tn), lambda i, j: (i, 0, j)),
            pl.BlockSpec((1, 1, tn), lambda i, j: (i, 0, j)),
        ],
        out_specs=pl.BlockSpec((1, 1, tn), lambda i, j: (i, 0, j)),
        compiler_params=_params("arbitrary", "arbitrary"),
        name="adaln",
    )(c.reshape(d, 1), ada_w, ada_b.reshape(depth, 1, n))
    return out.reshape(depth, 6, d)


def _rope_kernel(pos_ref, f64_ref, f256_ref, a_mask_ref, b_sign_ref, a_ref, b_ref, cos_ref, sin_ref):
    pos = pos_ref[...]
    ang = pos * f64_ref[...]
    a_ref[...] = jnp.cos(ang) * a_mask_ref[...]
    b_ref[...] = jnp.sin(ang) * b_sign_ref[...]
    ang = pos * f256_ref[...]
    cos_ref[...] = jnp.cos(ang)
    sin_ref[...] = jnp.sin(ang)


def _rope_tables(positions):
    s = positions.shape[-1]
    tm = 1024
    pos = positions.astype(F32).reshape(s, 1)
    f64 = ROPE_BASE ** (-jnp.arange(0, MLA_ROPE, 2, dtype=F32) / MLA_ROPE)
    f256 = ROPE_BASE ** (-jnp.arange(0, RET_KEY_DIM, 2, dtype=F32) / RET_KEY_DIM)
    quarter = MLA_ROPE // 2
    ones, zeros = jnp.ones((quarter,), F32), jnp.zeros((quarter,), F32)
    a_mask = jnp.concatenate([ones, ones, zeros, zeros]).reshape(1, LANES)
    b_sign = jnp.concatenate([-ones, ones, zeros, zeros]).reshape(1, LANES)
    row = pl.BlockSpec((tm, LANES), lambda i: (i, 0))
    return pl.pallas_call(
        _rope_kernel,
        out_shape=[jax.ShapeDtypeStruct((s, LANES), F32)] * 4,
        grid=(s // tm,),
        in_specs=[pl.BlockSpec((tm, 1), lambda i: (i, 0))] + [_full((1, LANES))] * 4,
        out_specs=[row] * 4,
        compiler_params=_params("arbitrary"),
        name="rope_tables",
    )(pos, jnp.tile(f64, 4).reshape(1, LANES), f256.reshape(1, LANES), a_mask, b_sign)


def _modulated(x, g, shift, scale):
    return (_rms(x, g) * (1.0 + scale) + shift).astype(BF16)


def _mla_in_kernel(x_ref, mod_ref, ng_ref, a_ref, b_ref, w_in_ref, qn_ref, kvn_ref, wq_ref, wk_ref, wv_ref,
                   q_ref, k_ref, v_ref, *, q_scale):
    h = _modulated(x_ref[...], ng_ref[0:1, :], mod_ref[0:1, :], mod_ref[1:2, :])
    lat = _dot(h, w_in_ref[...])
    rope_a, rope_b = a_ref[...], b_ref[...]

    def rope(v):
        return v * rope_a + pltpu.roll(v, 2 * (MLA_ROPE // 2), axis=1) * rope_b

    q_lat = lat[:, :MLA_Q_RANK]
    kv_lat = lat[:, MLA_Q_RANK:MLA_Q_RANK + MLA_KV_RANK]
    k_pe = rope(lat[:, MLA_Q_RANK + MLA_KV_RANK:]).astype(BF16)
    q = _dot(_rms(q_lat, qn_ref[...]).astype(BF16), wq_ref[...])
    kvn = _rms(kv_lat, kvn_ref[...]).astype(BF16)
    k_nope = _dot(kvn, wk_ref[...]).astype(BF16)
    v_t = _dot_nt(wv_ref[...], kvn).astype(BF16)
    v_ref[...] = v_t.reshape(MLA_HEADS, 1, MLA_V, v_t.shape[1])
    q_parts, k_parts = [], []
    for hd in range(MLA_HEADS):
        lo = hd * MLA_QK_PAD
        q_parts.append((q[:, lo:lo + MLA_NOPE] * q_scale).astype(BF16))
        q_parts.append((rope(q[:, lo + MLA_NOPE:lo + MLA_QK_PAD]) * q_scale).astype(BF16))
        k_parts.append(k_nope[:, hd * MLA_NOPE:(hd + 1) * MLA_NOPE])
        k_parts.append(k_pe)
    q_ref[...] = jnp.concatenate(q_parts, axis=1)
    k_ref[...] = jnp.concatenate(k_parts, axis=1)


def _mla_weights(w_in, w_q_up, w_kv_up):
    half = MLA_ROPE // 2
    pe = w_in[:, MLA_Q_RANK + MLA_KV_RANK:]
    pe1, pe2 = pe[:, :half], pe[:, half:]
    w_in_p = jnp.concatenate([w_in[:, :MLA_Q_RANK + MLA_KV_RANK], pe1, pe2, pe2, pe1], axis=1)
    wq = w_q_up.reshape(MLA_Q_RANK, MLA_HEADS, MLA_NOPE + MLA_ROPE)
    q1, q2 = wq[..., MLA_NOPE:MLA_NOPE + half], wq[..., MLA_NOPE + half:]
    wq_p = jnp.concatenate([wq[..., :MLA_NOPE], q1, q2, q2, q1], axis=-1).reshape(MLA_Q_RANK, MLA_HEADS * MLA_QK_PAD)
    wkv = w_kv_up.reshape(MLA_KV_RANK, MLA_HEADS, MLA_NOPE + MLA_V)
    wk = wkv[..., :MLA_NOPE].reshape(MLA_KV_RANK, MLA_HEADS * MLA_NOPE)
    wv_t = wkv[..., MLA_NOPE:].reshape(MLA_KV_RANK, MLA_HEADS * MLA_V).T
    return w_in_p.astype(BF16), wq_p.astype(BF16), wk.astype(BF16), wv_t.astype(BF16)


def _mla_in(x, mod, ng, rope_a, rope_b, w_in, q_norm, kv_norm, w_q_up, w_kv_up):
    s, d = x.shape
    tm = MLA_BLOCK
    w_in_p, wq_p, wk, wv_t = _mla_weights(w_in, w_q_up, w_kv_up)
    q_scale = (MLA_NOPE + MLA_ROPE) ** -0.5 * LOG2E
    row = lambda n: pl.BlockSpec((tm, n), lambda i: (i, 0))
    hq = MLA_HEADS * MLA_QK_PAD
    return pl.pallas_call(
        functools.partial(_mla_in_kernel, q_scale=q_scale),
        out_shape=[jax.ShapeDtypeStruct((s, hq), BF16), jax.ShapeDtypeStruct((s, hq), BF16),
                   jax.ShapeDtypeStruct((MLA_HEADS, s // tm, MLA_V, tm), BF16)],
        grid=(s // tm,),
        in_specs=[row(d), _full(mod.shape), _full(ng.shape), row(LANES), row(LANES), _full(w_in_p.shape),
                  _full((1, MLA_Q_RANK)), _full((1, MLA_KV_RANK)), _full(wq_p.shape), _full(wk.shape),
                  _full(wv_t.shape)],
        out_specs=[row(hq), row(hq), pl.BlockSpec((MLA_HEADS, 1, MLA_V, tm), lambda i: (0, i, 0, 0))],
        compiler_params=_params("arbitrary"),
        name="mla_in",
    )(x, mod, ng, rope_a, rope_b, w_in_p, q_norm.reshape(1, -1), kv_norm.reshape(1, -1), wq_p, wk, wv_t)


def _interleave(producers, consumers, lead=2):
    order = list(producers[:lead])
    rest = list(producers[lead:])
    for f in consumers:
        order.append(f)
        if rest:
            order.append(rest.pop(0))
    return order + rest


def _mla_attn_kernel(q_ref, k_ref, v_ref, o_ref, *scratch, bq, bk, sub):
    i = pl.program_id(1)
    groups = bq // sub
    assert bq == 2 * bk
    m_refs, l_refs = scratch[:groups], scratch[groups:2 * groups]
    acc_refs = scratch[2 * groups:3 * groups]
    s_refs = (scratch[3 * groups:4 * groups], scratch[4 * groups:5 * groups])
    for c in range(groups):
        m_refs[c][...] = jnp.full((1, sub), MASK_VALUE, F32)
        l_refs[c][...] = jnp.zeros((1, sub), F32)
        acc_refs[c][...] = jnp.zeros((MLA_V, sub), F32)

    def offset_of(c, diag):
        return None if diag is None else c * sub - diag * bk

    def live_groups(diag):
        return [c for c in range(groups) if diag is None or offset_of(c, diag) >= -(sub - 1)]

    def producers(slot, j, diag):
        start = pl.multiple_of(j * bk, bk)
        k = k_ref[pl.ds(start, bk), :]

        def make(c):
            def produce():
                s_refs[slot][c][...] = _dot_nt(k, q_ref[c * sub:(c + 1) * sub, :])
            return produce
        return [make(c) for c in live_groups(diag)]

    def consumers(slot, j, diag):
        v = v_ref[0, j]

        def make(c):
            offset = offset_of(c, diag)

            def consume():
                s = s_refs[slot][c][...]
                if offset is not None and offset < bk - 1:
                    key = lax.broadcasted_iota(jnp.int32, (bk, sub), 0)
                    qry = lax.broadcasted_iota(jnp.int32, (bk, sub), 1)
                    s = jnp.where(key - qry <= offset, s, MASK_VALUE)
                m_prev = m_refs[c][...]
                m_new = jnp.maximum(m_prev, jnp.max(s, axis=0, keepdims=True))
                alpha = jnp.exp2(m_prev - m_new)
                p = jnp.exp2(s - m_new)
                l_refs[c][...] = alpha * l_refs[c][...] + jnp.sum(p, axis=0, keepdims=True)
                m_refs[c][...] = m_new
                acc_refs[c][...] = alpha * acc_refs[c][...] + _dot(v, p.astype(BF16))
            return consume
        return [make(c) for c in live_groups(diag)]

    def stage(slot, j, diag, next_j, next_diag):
        nxt = [] if next_j is None else producers(1 - slot, next_j, next_diag)
        for f in _interleave(nxt, consumers(slot, j, diag)):
            f()

    for f in producers(0, 0, None):
        f()

    def body(t, carry):
        stage(0, 2 * t, None, 2 * t + 1, None)
        stage(1, 2 * t + 1, None, 2 * t + 2, None)
        return carry

    lax.fori_loop(0, i, body, 0)
    stage(0, 2 * i, 0, 2 * i + 1, 1)
    stage(1, 2 * i + 1, 1, None, None)
    for c in range(groups):
        o_ref[c * sub:(c + 1) * sub, :] = (acc_refs[c][...] / l_refs[c][...]).T.astype(o_ref.dtype)


def _mla_attn(q, k, v_t):
    s = q.shape[0]
    bq, bk, sub = MLA_Q_BLOCK, MLA_BLOCK, MLA_Q_GROUP
    groups = bq // sub
    return pl.pallas_call(
        functools.partial(_mla_attn_kernel, bq=bq, bk=bk, sub=sub),
        out_shape=jax.ShapeDtypeStruct((s, MLA_HEADS * MLA_V), BF16),
        grid=(MLA_HEADS, s // bq),
        in_specs=[pl.BlockSpec((bq, MLA_QK_PAD), lambda h, i: (i, h)),
                  pl.BlockSpec((s, MLA_QK_PAD), lambda h, i: (0, h)),
                  pl.BlockSpec((1, s // bk, MLA_V, bk), lambda h, i: (h, 0, 0, 0))],
        out_specs=pl.BlockSpec((bq, MLA_V), lambda h, i: (i, h)),
        scratch_shapes=([pltpu.VMEM((1, sub), F32)] * (2 * groups) + [pltpu.VMEM((MLA_V, sub), F32)] * groups
                        + [pltpu.VMEM((bk, sub), F32)] * (2 * groups)),
        compiler_params=_params("arbitrary", "arbitrary"),
        name="mla_attn",
    )(q, k, v_t)


def _mla_mixer(x, mod, ng, tables, w_in, q_norm, kv_norm, w_q_up, w_kv_up):
    q, k, v = _mla_in(x, mod, ng, tables[0], tables[1], w_in, q_norm, kv_norm, w_q_up, w_kv_up)
    return _mla_attn(q, k, v)


def _sb_in_kernel(x_ref, mod_ref, ng_ref, w_ref, q_ref, k_ref, v_ref, *, q_scale):
    h = _modulated(x_ref[...], ng_ref[0:1, :], mod_ref[0:1, :], mod_ref[1:2, :])
    qkv = _dot(h, w_ref[...])
    n = SB_HEADS * SB_HEAD_DIM
    q_ref[...] = (qkv[:, :n] * q_scale).astype(BF16)
    k_ref[...] = qkv[:, n:2 * n].astype(BF16)
    v_ref[...] = qkv[:, 2 * n:].astype(BF16)


def _sb_in(x, mod, ng, w_in):
    s, d = x.shape
    tm = ROW_TILE
    n = SB_HEADS * SB_HEAD_DIM
    row = lambda w: pl.BlockSpec((tm, w), lambda i: (i, 0))
    w = w_in.astype(BF16)
    return pl.pallas_call(
        functools.partial(_sb_in_kernel, q_scale=-LOG2E * SB_HEAD_DIM ** -0.5),
        out_shape=[jax.ShapeDtypeStruct((s, n), BF16)] * 3,
        grid=(s // tm,),
        in_specs=[row(d), _full(mod.shape), _full(ng.shape), _full(w.shape)],
        out_specs=[row(n)] * 3,
        compiler_params=_params("arbitrary"),
        name="sb_in",
    )(x, mod, ng, w)


def _sb_attn_kernel(q_ref, k_ref, v_ref, tri_ref, o_ref, acc_ref, *scratch, bq, bk, heads):
    i = pl.program_id(1)
    assert bq == 2 * bk
    width = heads * SB_HEAD_DIM
    c_refs = scratch[:heads]
    w_refs = (scratch[heads:2 * heads], scratch[2 * heads:3 * heads])
    q = q_ref[...]
    lane_q = lax.broadcasted_iota(jnp.int32, q.shape, 1)
    lane_v = lax.broadcasted_iota(jnp.int32, (bk, width), 1)

    def head_lanes(lane, hd):
        return (lane >= hd * SB_HEAD_DIM) & (lane < (hd + 1) * SB_HEAD_DIM)

    q_heads = [jnp.where(head_lanes(lane_q, hd), q, jnp.zeros_like(q)) for hd in range(heads)]
    acc_ref[...] = jnp.zeros(acc_ref.shape, F32)
    for hd in range(heads):
        c_refs[hd][...] = jnp.zeros((bq, LANES), F32)
    tri = tri_ref[...]

    def producers(slot, j):
        start = pl.multiple_of(j * bk, bk)
        k = k_ref[pl.ds(start, bk), :]

        def make(hd):
            def produce():
                w_refs[slot][hd][...] = _dot_nt(q_heads[hd], k)
            return produce
        return [make(hd) for hd in range(heads)]

    def stage(slot, j, mask_offset, next_j):
        start = pl.multiple_of(j * bk, bk)
        v = v_ref[pl.ds(start, bk), :]
        if mask_offset is not None:
            row = lax.broadcasted_iota(jnp.int32, (bq, bk), 0)
            col = lax.broadcasted_iota(jnp.int32, (bq, bk), 1)
            strict = (col - row) < mask_offset
        weights = [None] * heads

        def make(hd):
            def consume():
                w = w_refs[slot][hd][...]
                neg_log_b = jnp.log2(1.0 + jnp.exp2(jnp.minimum(w, SB_EXP2_CLAMP)))
                log_1m = jnp.minimum(w - neg_log_b, 0.0)
                if mask_offset is not None:
                    log_1m = jnp.where(strict, log_1m, 0.0)
                suffix = _dot(log_1m.astype(BF16), tri)
                c = c_refs[hd][...]
                a = jnp.exp2(suffix - neg_log_b + _tile_lanes(c, bk))
                if mask_offset is not None:
                    a = jnp.where(strict, a, 0.0)
                weights[hd] = a.astype(BF16)
                c_refs[hd][...] = c + jnp.sum(log_1m, axis=1, keepdims=True)
            return consume

        nxt = [] if next_j is None else producers(1 - slot, next_j)
        for f in _interleave(nxt, [make(hd) for hd in range(heads)]):
            f()
        v_heads = [jnp.where(head_lanes(lane_v, hd), v, jnp.zeros_like(v)) for hd in range(heads)]
        acc_ref[...] += _dot(jnp.concatenate(weights, axis=1), jnp.concatenate(v_heads, axis=0))

    for f in producers(0, 2 * i + 1):
        f()
    stage(0, 2 * i + 1, -bk, 2 * i)
    stage(1, 2 * i, 0, jnp.maximum(2 * i - 1, 0))

    def body(t, carry):
        j = 2 * i - 1 - 2 * t
        stage(0, j, None, j - 1)
        stage(1, j - 1, None, jnp.maximum(j - 2, 0))
        return carry

    lax.fori_loop(0, i, body, 0)
    o_ref[...] = acc_ref[...].astype(o_ref.dtype)


def _sb_attn(q, k, v):
    s = q.shape[0]
    bq, bk, heads = SB_Q_BLOCK, SB_K_BLOCK, SB_HEADS_PER_STEP
    width = heads * SB_HEAD_DIM
    tri = jnp.tril(jnp.ones((bk, bk), F32), k=-1).astype(BF16)
    resident = lambda: pl.BlockSpec((s, width), lambda p, i: (0, p), pipeline_mode=pl.Buffered(1))
    return pl.pallas_call(
        functools.partial(_sb_attn_kernel, bq=bq, bk=bk, heads=heads),
        out_shape=jax.ShapeDtypeStruct((s, SB_HEADS * SB_HEAD_DIM), BF16),
        grid=(SB_HEADS // heads, s // bq),
        in_specs=[pl.BlockSpec((bq, width), lambda p, i: (i, p)), resident(), resident(), _full((bk, bk))],
        out_specs=pl.BlockSpec((bq, width), lambda p, i: (i, p)),
        scratch_shapes=([pltpu.VMEM((bq, width), F32)] + [pltpu.VMEM((bq, LANES), F32)] * heads
                        + [pltpu.VMEM((bq, bk), F32)] * (2 * heads)),
        compiler_params=_params("arbitrary", "arbitrary"),
        name="sb_attn",
    )(q, k, v, tri)


def _ret_in_kernel(x_ref, mod_ref, ng_ref, cos_ref, sin_ref, w_ref, q_ref, k_ref, v_ref, g_ref, *, k_scale):
    h = _modulated(x_ref[...], ng_ref[0:1, :], mod_ref[0:1, :], mod_ref[1:2, :])
    y = _dot(h, w_ref[...])
    hk, hv = RET_HEADS * RET_KEY_DIM, RET_HEADS * RET_VAL_DIM
    cos, sin = cos_ref[...], sin_ref[...]
    half = RET_KEY_DIM // 2

    def rope(t, scale):
        parts = []
        for hd in range(RET_HEADS):
            t1 = t[:, hd * RET_KEY_DIM:hd * RET_KEY_DIM + half]
            t2 = t[:, hd * RET_KEY_DIM + half:(hd + 1) * RET_KEY_DIM]
            parts.append(((t1 * cos - t2 * sin) * scale).astype(BF16))
            parts.append(((t2 * cos + t1 * sin) * scale).astype(BF16))
        return jnp.concatenate(parts, axis=1)

    q_ref[...] = rope(y[:, :hk], 1.0)
    k_ref[...] = rope(y[:, hk:2 * hk], k_scale)
    v_ref[...] = y[:, 2 * hk:2 * hk + hv].astype(BF16)
    g_ref[...] = y[:, 2 * hk + hv:]


def _ret_in(x, mod, ng, cos, sin, w_in):
    s, d = x.shape
    tm = ROW_TILE
    hk, hv = RET_HEADS * RET_KEY_DIM, RET_HEADS * RET_VAL_DIM
    row = lambda w: pl.BlockSpec((tm, w), lambda i: (i, 0))
    w = w_in.astype(BF16)
    return pl.pallas_call(
        functools.partial(_ret_in_kernel, k_scale=RET_KEY_DIM ** -0.5),
        out_shape=[jax.ShapeDtypeStruct((s, hk), BF16), jax.ShapeDtypeStruct((s, hk), BF16),
                   jax.ShapeDtypeStruct((s, hv), BF16), jax.ShapeDtypeStruct((s, hv), F32)],
        grid=(s // tm,),
        in_specs=[row(d), _full(mod.shape), _full(ng.shape), row(LANES), row(LANES), _full(w.shape)],
        out_specs=[row(hk), row(hk), row(hv), row(hv)],
        compiler_params=_params("arbitrary"),
        name="ret_in",
    )(x, mod, ng, cos, sin, w)


def _ret_core_kernel(lg_ref, q_ref, k_ref, v_ref, g_ref, gn_ref, o_ref, state_ref, *, chunk):
    @pl.when(pl.program_id(0) == 0)
    def _():
        state_ref[...] = jnp.zeros(state_ref.shape, F32)

    row = lax.broadcasted_iota(jnp.int32, (chunk, chunk), 0)
    col = lax.broadcasted_iota(jnp.int32, (chunk, chunk), 1)
    diff = (row - col).astype(F32)
    idx = lax.broadcasted_iota(jnp.int32, (chunk, LANES), 0).astype(F32)
    for hd in range(RET_HEADS):
        lg = lg_ref[hd]
        lg_row = lg[0:1, :]
        decay = jnp.where(diff >= 0.0, jnp.exp(jnp.maximum(diff, 0.0) * _tile_lanes(lg_row, chunk)), 0.0)
        xi = jnp.exp((idx + 1.0) * lg_row)
        zeta = jnp.exp((chunk - 1.0 - idx) * lg_row)
        g_chunk = jnp.exp(chunk * lg)
        q = q_ref[:, hd * RET_KEY_DIM:(hd + 1) * RET_KEY_DIM]
        k = k_ref[:, hd * RET_KEY_DIM:(hd + 1) * RET_KEY_DIM]
        v = v_ref[:, hd * RET_VAL_DIM:(hd + 1) * RET_VAL_DIM]
        state = state_ref[hd]
        sc = _dot_nt(q, k) * decay
        inner = _dot(sc.astype(BF16), v)
        cross = _dot(q, state.astype(BF16)) * _tile_lanes(xi, RET_VAL_DIM)
        kz = (k.astype(F32) * _tile_lanes(zeta, RET_KEY_DIM)).astype(BF16)
        upd = lax.dot_general(kz, v, (((0,), (0,)), ((), ())), preferred_element_type=F32)
        state_ref[hd] = state * _tile_lanes(g_chunk[0:1, :], RET_VAL_DIM) + upd
        o = inner + cross
        o = o * lax.rsqrt(jnp.mean(o * o, axis=-1, keepdims=True) + EPS)
        o = o * gn_ref[:, hd * RET_VAL_DIM:(hd + 1) * RET_VAL_DIM]
        g = g_ref[:, hd * RET_VAL_DIM:(hd + 1) * RET_VAL_DIM]
        o_ref[:, hd * RET_VAL_DIM:(hd + 1) * RET_VAL_DIM] = (g * (1.0 / (1.0 + jnp.exp(-g))) * o).astype(o_ref.dtype)


def _ret_core(q, k, v, g, gn_g):
    s = q.shape[0]
    chunk = RET_CHUNK
    hk, hv = RET_HEADS * RET_KEY_DIM, RET_HEADS * RET_VAL_DIM
    log_gamma = jnp.log(1.0 - 2.0 ** (-5.0 - jnp.arange(RET_HEADS, dtype=F32)))
    lg = jnp.broadcast_to(log_gamma[:, None, None], (RET_HEADS, SUBLANES, LANES))
    row = lambda w: pl.BlockSpec((chunk, w), lambda i: (i, 0))
    return pl.pallas_call(
        functools.partial(_ret_core_kernel, chunk=chunk),
        out_shape=jax.ShapeDtypeStruct((s, hv), BF16),
        grid=(s // chunk,),
        in_specs=[_full(lg.shape), row(hk), row(hk), row(hv), row(hv), _full((1, hv))],
        out_specs=row(hv),
        scratch_shapes=[pltpu.VMEM((RET_HEADS, RET_KEY_DIM, RET_VAL_DIM), F32)],
        compiler_params=_params("arbitrary"),
        name="ret_core",
    )(lg, q, k, v, g, gn_g.reshape(1, hv))


def _post_kernel(o_ref, x_ref, mod_ref, ng_ref, wo_ref, w1_ref, w2_ref, out_ref, y_ref):
    x = x_ref[...]
    y = _dot(o_ref[...], wo_ref[...])
    x = x + mod_ref[2:3, :] * _rms(y, ng_ref[1:2, :])
    h = _modulated(x, ng_ref[2:3, :], mod_ref[3:4, :], mod_ref[4:5, :])
    d_ff = w1_ref.shape[1]
    for c in range(d_ff // FF_CHUNK):
        a = jnp.maximum(_dot(h, w1_ref[:, c * FF_CHUNK:(c + 1) * FF_CHUNK]), 0.0)
        part = _dot((a * a).astype(BF16), w2_ref[c * FF_CHUNK:(c + 1) * FF_CHUNK, :])
        if c == 0:
            y_ref[...] = part
        else:
            y_ref[...] += part
    out_ref[...] = x + mod_ref[5:6, :] * _rms(y_ref[...], ng_ref[3:4, :])


def _post(o, x, mod, ng, w_out, w1, w2):
    s, d = x.shape
    tm = ROW_TILE
    ko = o.shape[1]
    row = lambda w: pl.BlockSpec((tm, w), lambda i: (i, 0))
    resident = lambda a: pl.BlockSpec(a.shape, lambda i: (0, 0), pipeline_mode=pl.Buffered(1))
    wo, w1b, w2b = w_out.astype(BF16), w1.astype(BF16), w2.astype(BF16)
    return pl.pallas_call(
        _post_kernel,
        out_shape=jax.ShapeDtypeStruct((s, d), F32),
        grid=(s // tm,),
        in_specs=[row(ko), row(d), _full(mod.shape), _full(ng.shape), resident(wo), resident(w1b), resident(w2b)],
        out_specs=row(d),
        scratch_shapes=[pltpu.VMEM((tm, d), F32)],
        compiler_params=_params("arbitrary"),
        name="post",
    )(o, x, mod, ng, wo, w1b, w2b)


def kernel(x, c, positions, ada_w, ada_b, norm_g, ffn_w1, ffn_w2, mla_w_in, mla_q_norm, mla_kv_norm, mla_w_q_up,
           mla_w_kv_up, mla_w_out, sb_w_in, sb_w_out, ret_w_in, ret_gn_g, ret_w_out):
    batch, seq, d = x.shape
    assert batch == 1 and seq % MLA_Q_BLOCK == 0 and seq % ROW_TILE == 0
    depth = ada_w.shape[0]
    mod = _adaln(c, ada_w, ada_b)
    tables = _rope_tables(positions)
    xs = x.reshape(seq, d)
    for i in range(depth):
        kind, j = i % 3, i // 3
        if kind == 0:
            o = _mla_mixer(xs, mod[i], norm_g[i], tables, mla_w_in[j], mla_q_norm[j], mla_kv_norm[j],
                           mla_w_q_up[j], mla_w_kv_up[j])
            w_out = mla_w_out[j]
        elif kind == 1:
            q, k, v = _sb_in(xs, mod[i], norm_g[i], sb_w_in[j])
            o, w_out = _sb_attn(q, k, v), sb_w_out[j]
        else:
            q, k, v, g = _ret_in(xs, mod[i], norm_g[i], tables[2], tables[3], ret_w_in[j])
            o, w_out = _ret_core(q, k, v, g, ret_gn_g[j]), ret_w_out[j]
        xs = _post(o, xs, mod[i], norm_g[i], w_out, ffn_w1[i], ffn_w2[i])
    return xs.reshape(batch, seq, d)
```

```python
import functools
import math

import jax
import jax.numpy as jnp
from jax import lax
from jax.experimental import pallas as pl
from jax.experimental.pallas import tpu as pltpu

EPS = 1e-6
ROPE_BASE = 10000.0
MASK_VALUE = -1e30
LOG2E = math.log2(math.e)

LANES = 128
SUBLANES = 8
VMEM_LIMIT_BYTES = 60000 * 1024

MLA_HEADS = 8
MLA_Q_RANK = 256
MLA_KV_RANK = 128
MLA_NOPE = 128
MLA_ROPE = 64
MLA_V = 128
MLA_QK_PAD = 256

SB_HEADS = 16
SB_HEAD_DIM = 64

RET_HEADS = 4
RET_KEY_DIM = 256
RET_VAL_DIM = 512

ROW_TILE = 512
FF_CHUNK = 1024
MLA_BLOCK = 512
MLA_Q_BLOCK = 1024
MLA_Q_GROUP = 256
SB_Q_BLOCK = 512
SB_K_BLOCK = 256
SB_HEADS_PER_STEP = 4
SB_EXP2_CLAMP = 126.0
SB_EXP2_UNDERFLOW = 160.0
RET_CHUNK = 256

F32 = jnp.float32
BF16 = jnp.bfloat16


def _params(*semantics):
    return pltpu.CompilerParams(dimension_semantics=semantics, vmem_limit_bytes=VMEM_LIMIT_BYTES)


def _full(shape):
    nd = len(shape)
    return pl.BlockSpec(shape, lambda *_: (0,) * nd)


def _rms(x, g):
    return x * lax.rsqrt(jnp.mean(x * x, axis=-1, keepdims=True) + EPS) * g


def _dot(a, b):
    return jnp.dot(a, b, preferred_element_type=F32)


def _dot_nt(a, b):
    return lax.dot_general(a, b, (((1,), (1,)), ((), ())), preferred_element_type=F32)


def _tile_lanes(v, width):
    reps = width // LANES
    return v if reps == 1 else jnp.concatenate([v] * reps, axis=1)


def _adaln_kernel(c_ref, w_ref, b_ref, o_ref):
    c = c_ref[...]
    cond = c * (1.0 / (1.0 + jnp.exp(-c)))
    o_ref[0] = jnp.sum(w_ref[0] * cond, axis=0, keepdims=True) + b_ref[0]


def _adaln(c, ada_w, ada_b):
    depth, d, n = ada_w.shape
    tn = 768
    out = pl.pallas_call(
        _adaln_kernel,
        out_shape=jax.ShapeDtypeStruct((depth, 1, n), F32),
        grid=(depth, n // tn),
        in_specs=[
            _full((d, 1)),
            pl.BlockSpec((1, d, tn), lambda i, j: (i, 0, j)),
            pl.BlockSpec((1, 1, tn), lambda i, j: (i, 0, j)),
        ],
        out_specs=pl.BlockSpec((1, 1, tn), lambda i, j: (i, 0, j)),
        compiler_params=_params("arbitrary", "arbitrary"),
        name="adaln",
    )(c.reshape(d, 1), ada_w, ada_b.reshape(depth, 1, n))
    return out.reshape(depth, 6, d)


def _rope_kernel(pos_ref, f64_ref, f256_ref, a_mask_ref, b_sign_ref, a_ref, b_ref, cos_ref, sin_ref):
    pos = pos_ref[...]
    ang = pos * f64_ref[...]
    a_ref[...] = jnp.cos(ang) * a_mask_ref[...]
    b_ref[...] = jnp.sin(ang) * b_sign_ref[...]
    ang = pos * f256_ref[...]
    cos_ref[...] = jnp.cos(ang)
    sin_ref[...] = jnp.sin(ang)


def _rope_tables(positions):
    s = positions.shape[-1]
    tm = 1024
    pos = positions.astype(F32).reshape(s, 1)
    f64 = ROPE_BASE ** (-jnp.arange(0, MLA_ROPE, 2, dtype=F32) / MLA_ROPE)
    f256 = ROPE_BASE ** (-jnp.arange(0, RET_KEY_DIM, 2, dtype=F32) / RET_KEY_DIM)
    quarter = MLA_ROPE // 2
    ones, zeros = jnp.ones((quarter,), F32), jnp.zeros((quarter,), F32)
    a_mask = jnp.concatenate([ones, ones, zeros, zeros]).reshape(1, LANES)
    b_sign = jnp.concatenate([-ones, ones, zeros, zeros]).reshape(1, LANES)
    row = pl.BlockSpec((tm, LANES), lambda i: (i, 0))
    return pl.pallas_call(
        _rope_kernel,
        out_shape=[jax.ShapeDtypeStruct((s, LANES), F32)] * 4,
        grid=(s // tm,),
        in_specs=[pl.BlockSpec((tm, 1), lambda i: (i, 0))] + [_full((1, LANES))] * 4,
        out_specs=[row] * 4,
        compiler_params=_params("arbitrary"),
        name="rope_tables",
    )(pos, jnp.tile(f64, 4).reshape(1, LANES), f256.reshape(1, LANES), a_mask, b_sign)


def _modulated(x, g, shift, scale):
    return (_rms(x, g) * (1.0 + scale) + shift).astype(BF16)


def _mla_in_kernel(x_ref, mod_ref, ng_ref, a_ref, b_ref, w_in_ref, qn_ref, kvn_ref, wq_ref, wk_ref, wv_ref,
                   q_ref, k_ref, v_ref, *, q_scale):
    h = _modulated(x_ref[...], ng_ref[0:1, :], mod_ref[0:1, :], mod_ref[1:2, :])
    lat = _dot(h, w_in_ref[...])
    rope_a, rope_b = a_ref[...], b_ref[...]

    def rope(v):
        return v * rope_a + pltpu.roll(v, 2 * (MLA_ROPE // 2), axis=1) * rope_b

    q_lat = lat[:, :MLA_Q_RANK]
    kv_lat = lat[:, MLA_Q_RANK:MLA_Q_RANK + MLA_KV_RANK]
    k_pe = rope(lat[:, MLA_Q_RANK + MLA_KV_RANK:]).astype(BF16)
    q = _dot(_rms(q_lat, qn_ref[...]).astype(BF16), wq_ref[...])
    kvn = _rms(kv_lat, kvn_ref[...]).astype(BF16)
    k_nope = _dot(kvn, wk_ref[...]).astype(BF16)
    v_t = _dot_nt(wv_ref[...], kvn).astype(BF16)
    v_ref[...] = v_t.reshape(MLA_HEADS, 1, MLA_V, v_t.shape[1])
    q_parts, k_parts = [], []
    for hd in range(MLA_HEADS):
        lo = hd * MLA_QK_PAD
        q_parts.append((q[:, lo:lo + MLA_NOPE] * q_scale).astype(BF16))
        q_parts.append((rope(q[:, lo + MLA_NOPE:lo + MLA_QK_PAD]) * q_scale).astype(BF16))
        k_parts.append(k_nope[:, hd * MLA_NOPE:(hd + 1) * MLA_NOPE])
        k_parts.append(k_pe)
    q_ref[...] = jnp.concatenate(q_parts, axis=1)
    k_ref[...] = jnp.concatenate(k_parts, axis=1)


def _mla_weights(w_in, w_q_up, w_kv_up):
    half = MLA_ROPE // 2
    pe = w_in[:, MLA_Q_RANK + MLA_KV_RANK:]
    pe1, pe2 = pe[:, :half], pe[:, half:]
    w_in_p = jnp.concatenate([w_in[:, :MLA_Q_RANK + MLA_KV_RANK], pe1, pe2, pe2, pe1], axis=1)
    wq = w_q_up.reshape(MLA_Q_RANK, MLA_HEADS, MLA_NOPE + MLA_ROPE)
    q1, q2 = wq[..., MLA_NOPE:MLA_NOPE + half], wq[..., MLA_NOPE + half:]
    wq_p = jnp.concatenate([wq[..., :MLA_NOPE], q1, q2, q2, q1], axis=-1).reshape(MLA_Q_RANK, MLA_HEADS * MLA_QK_PAD)
    wkv = w_kv_up.reshape(MLA_KV_RANK, MLA_HEADS, MLA_NOPE + MLA_V)
    wk = wkv[..., :MLA_NOPE].reshape(MLA_KV_RANK, MLA_HEADS * MLA_NOPE)
    wv_t = wkv[..., MLA_NOPE:].reshape(MLA_KV_RANK, MLA_HEADS * MLA_V).T
    return w_in_p.astype(BF16), wq_p.astype(BF16), wk.astype(BF16), wv_t.astype(BF16)


def _mla_in(x, mod, ng, rope_a, rope_b, w_in, q_norm, kv_norm, w_q_up, w_kv_up):
    s, d = x.shape
    tm = MLA_BLOCK
    w_in_p, wq_p, wk, wv_t = _mla_weights(w_in, w_q_up, w_kv_up)
    q_scale = (MLA_NOPE + MLA_ROPE) ** -0.5 * LOG2E
    row = lambda n: pl.BlockSpec((tm, n), lambda i: (i, 0))
    hq = MLA_HEADS * MLA_QK_PAD
    return pl.pallas_call(
        functools.partial(_mla_in_kernel, q_scale=q_scale),
        out_shape=[jax.ShapeDtypeStruct((s, hq), BF16), jax.ShapeDtypeStruct((s, hq), BF16),
                   jax.ShapeDtypeStruct((MLA_HEADS, s // tm, MLA_V, tm), BF16)],
        grid=(s // tm,),
        in_specs=[row(d), _full(mod.shape), _full(ng.shape), row(LANES), row(LANES), _full(w_in_p.shape),
                  _full((1, MLA_Q_RANK)), _full((1, MLA_KV_RANK)), _full(wq_p.shape), _full(wk.shape),
                  _full(wv_t.shape)],
        out_specs=[row(hq), row(hq), pl.BlockSpec((MLA_HEADS, 1, MLA_V, tm), lambda i: (0, i, 0, 0))],
        compiler_params=_params("arbitrary"),
        name="mla_in",
    )(x, mod, ng, rope_a, rope_b, w_in_p, q_norm.reshape(1, -1), kv_norm.reshape(1, -1), wq_p, wk, wv_t)


def _interleave(producers, consumers, lead=2):
    order = list(producers[:lead])
    rest = list(producers[lead:])
    for f in consumers:
        order.append(f)
        if rest:
            order.append(rest.pop(0))
    return order + rest


def _mla_attn_kernel(q_ref, k_ref, v_ref, o_ref, *scratch, bq, bk, sub):
    i = pl.program_id(1)
    groups = bq // sub
    assert bq == 2 * bk
    m_refs, l_refs = scratch[:groups], scratch[groups:2 * groups]
    acc_refs = scratch[2 * groups:3 * groups]
    s_refs = (scratch[3 * groups:4 * groups], scratch[4 * groups:5 * groups])
    for c in range(groups):
        m_refs[c][...] = jnp.full((1, sub), MASK_VALUE, F32)
        l_refs[c][...] = jnp.zeros((1, sub), F32)
        acc_refs[c][...] = jnp.zeros((MLA_V, sub), F32)

    def offset_of(c, diag):
        return None if diag is None else c * sub - diag * bk

    def live_groups(diag):
        return [c for c in range(groups) if diag is None or offset_of(c, diag) >= -(sub - 1)]

    def producers(slot, j, diag):
        start = pl.multiple_of(j * bk, bk)
        k = k_ref[pl.ds(start, bk), :]

        def make(c):
            def produce():
                s_refs[slot][c][...] = _dot_nt(k, q_ref[c * sub:(c + 1) * sub, :])
            return produce
        return [make(c) for c in live_groups(diag)]

    def consumers(slot, j, diag):
        v = v_ref[0, j]

        def make(c):
            offset = offset_of(c, diag)

            def consume():
                s = s_refs[slot][c][...]
                if offset is not None and offset < bk - 1:
                    key = lax.broadcasted_iota(jnp.int32, (bk, sub), 0)
                    qry = lax.broadcasted_iota(jnp.int32, (bk, sub), 1)
                    s = jnp.where(key - qry <= offset, s, MASK_VALUE)
                m_prev = m_refs[c][...]
                m_new = jnp.maximum(m_prev, jnp.max(s, axis=0, keepdims=True))
                alpha = jnp.exp2(m_prev - m_new)
                p = jnp.exp2(s - m_new)
                l_refs[c][...] = alpha * l_refs[c][...] + jnp.sum(p, axis=0, keepdims=True)
                m_refs[c][...] = m_new
                acc_refs[c][...] = alpha * acc_refs[c][...] + _dot(v, p.astype(BF16))
            return consume
        return [make(c) for c in live_groups(diag)]

    def stage(slot, j, diag, next_j, next_diag):
        nxt = [] if next_j is None else producers(1 - slot, next_j, next_diag)
        for f in _interleave(nxt, consumers(slot, j, diag)):
            f()

    for f in producers(0, 0, None):
        f()

    def body(t, carry):
        stage(0, 2 * t, None, 2 * t + 1, None)
        stage(1, 2 * t + 1, None, 2 * t + 2, None)
        return carry

    lax.fori_loop(0, i, body, 0)
    stage(0, 2 * i, 0, 2 * i + 1, 1)
    stage(1, 2 * i + 1, 1, None, None)
    for c in range(groups):
        o_ref[c * sub:(c + 1) * sub, :] = (acc_refs[c][...] / l_refs[c][...]).T.astype(o_ref.dtype)


def _mla_attn(q, k, v_t):
    s = q.shape[0]
    bq, bk, sub = MLA_Q_BLOCK, MLA_BLOCK, MLA_Q_GROUP
    groups = bq // sub
    return pl.pallas_call(
        functools.partial(_mla_attn_kernel, bq=bq, bk=bk, sub=sub),
        out_shape=jax.ShapeDtypeStruct((s, MLA_HEADS * MLA_V), BF16),
        grid=(MLA_HEADS, s // bq),
        in_specs=[pl.BlockSpec((bq, MLA_QK_PAD), lambda h, i: (i, h)),
                  pl.BlockSpec((s, MLA_QK_PAD), lambda h, i: (0, h)),
                  pl.BlockSpec((1, s // bk, MLA_V, bk), lambda h, i: (h, 0, 0, 0))],
        out_specs=pl.BlockSpec((bq, MLA_V), lambda h, i: (i, h)),
        scratch_shapes=([pltpu.VMEM((1, sub), F32)] * (2 * groups) + [pltpu.VMEM((MLA_V, sub), F32)] * groups
                        + [pltpu.VMEM((bk, sub), F32)] * (2 * groups)),
        compiler_params=_params("arbitrary", "arbitrary"),
        name="mla_attn",
    )(q, k, v_t)


def _mla_mixer(x, mod, ng, tables, w_in, q_norm, kv_norm, w_q_up, w_kv_up):
    q, k, v = _mla_in(x, mod, ng, tables[0], tables[1], w_in, q_norm, kv_norm, w_q_up, w_kv_up)
    return _mla_attn(q, k, v)


def _sb_in_kernel(x_ref, mod_ref, ng_ref, w_ref, q_ref, k_ref, v_ref, *, q_scale):
    h = _modulated(x_ref[...], ng_ref[0:1, :], mod_ref[0:1, :], mod_ref[1:2, :])
    qkv = _dot(h, w_ref[...])
    n = SB_HEADS * SB_HEAD_DIM
    q_ref[...] = (qkv[:, :n] * q_scale).astype(BF16)
    k_ref[...] = qkv[:, n:2 * n].astype(BF16)
    v_ref[...] = qkv[:, 2 * n:].astype(BF16)


def _sb_in(x, mod, ng, w_in):
    s, d = x.shape
    tm = ROW_TILE
    n = SB_HEADS * SB_HEAD_DIM
    row = lambda w: pl.BlockSpec((tm, w), lambda i: (i, 0))
    w = w_in.astype(BF16)
    return pl.pallas_call(
        functools.partial(_sb_in_kernel, q_scale=-LOG2E * SB_HEAD_DIM ** -0.5),
        out_shape=[jax.ShapeDtypeStruct((s, n), BF16)] * 3,
        grid=(s // tm,),
        in_specs=[row(d), _full(mod.shape), _full(ng.shape), _full(w.shape)],
        out_specs=[row(n)] * 3,
        compiler_params=_params("arbitrary"),
        name="sb_in",
    )(x, mod, ng, w)


def _sb_attn_kernel(q_ref, k_ref, v_ref, tri_ref, o_ref, acc_ref, *scratch, bq, bk, heads):
    i = pl.program_id(1)
    assert bq == 2 * bk
    width = heads * SB_HEAD_DIM
    c_refs = scratch[:heads]
    w_refs = (scratch[heads:2 * heads], scratch[2 * heads:3 * heads])
    q = q_ref[...]
    lane_q = lax.broadcasted_iota(jnp.int32, q.shape, 1)
    lane_v = lax.broadcasted_iota(jnp.int32, (bk, width), 1)

    def head_lanes(lane, hd):
        return (lane >= hd * SB_HEAD_DIM) & (lane < (hd + 1) * SB_HEAD_DIM)

    q_heads = [jnp.where(head_lanes(lane_q, hd), q, jnp.zeros_like(q)) for hd in range(heads)]
    acc_ref[...] = jnp.zeros(acc_ref.shape, F32)
    for hd in range(heads):
        c_refs[hd][...] = jnp.zeros((bq, LANES), F32)
    tri = tri_ref[...]

    def producers(slot, j):
        start = pl.multiple_of(j * bk, bk)
        k = k_ref[pl.ds(start, bk), :]

        def make(hd):
            def produce():
                w_refs[slot][hd][...] = _dot_nt(q_heads[hd], k)
            return produce
        return [make(hd) for hd in range(heads)]

    def stage(slot, j, mask_offset, next_j):
        start = pl.multiple_of(j * bk, bk)
        v = v_ref[pl.ds(start, bk), :]
        if mask_offset is not None:
            row = lax.broadcasted_iota(jnp.int32, (bq, bk), 0)
            col = lax.broadcasted_iota(jnp.int32, (bq, bk), 1)
            strict = (col - row) < mask_offset
        weights = [None] * heads

        def make(hd):
            def consume():
                w = w_refs[slot][hd][...]
                neg_log_b = jnp.log2(1.0 + jnp.exp2(jnp.minimum(w, SB_EXP2_CLAMP)))
                log_1m = jnp.minimum(w - neg_log_b, 0.0)
                if mask_offset is not None:
                    log_1m = jnp.where(strict, log_1m, 0.0)
                suffix = _dot(log_1m.astype(BF16), tri)
                c = c_refs[hd][...]
                a = jnp.exp2(suffix - neg_log_b + _tile_lanes(c, bk))
                if mask_offset is not None:
                    a = jnp.where(strict, a, 0.0)
                weights[hd] = a.astype(BF16)
                c_refs[hd][...] = c + jnp.sum(log_1m, axis=1, keepdims=True)
            return consume

        nxt = [] if next_j is None else producers(1 - slot, next_j)
        for f in _interleave(nxt, [make(hd) for hd in range(heads)]):
            f()
        v_heads = [jnp.where(head_lanes(lane_v, hd), v, jnp.zeros_like(v)) for hd in range(heads)]
        acc_ref[...] += _dot(jnp.concatenate(weights, axis=1), jnp.concatenate(v_heads, axis=0))

    for f in producers(0, 2 * i + 1):
        f()
    stage(0, 2 * i + 1, -bk, 2 * i)
    stage(1, 2 * i, 0, jnp.maximum(2 * i - 1, 0))

    def any_weight_left():
        worst = c_refs[0][...]
        for hd in range(1, heads):
            worst = jnp.maximum(worst, c_refs[hd][...])
        return (jnp.max(worst) > -SB_EXP2_UNDERFLOW).astype(jnp.int32)

    def more(carry):
        t, left = carry
        return jnp.logical_and(t < i, left > 0)

    def body(carry):
        t, _ = carry
        j = 2 * i - 1 - 2 * t
        stage(0, j, None, j - 1)
        stage(1, j - 1, None, jnp.maximum(j - 2, 0))
        return t + 1, any_weight_left()

    lax.while_loop(more, body, (0, any_weight_left()))
    o_ref[...] = acc_ref[...].astype(o_ref.dtype)


def _sb_attn(q, k, v):
    s = q.shape[0]
    bq, bk, heads = SB_Q_BLOCK, SB_K_BLOCK, SB_HEADS_PER_STEP
    width = heads * SB_HEAD_DIM
    tri = jnp.tril(jnp.ones((bk, bk), F32), k=-1).astype(BF16)
    resident = lambda: pl.BlockSpec((s, width), lambda p, i: (0, p), pipeline_mode=pl.Buffered(1))
    return pl.pallas_call(
        functools.partial(_sb_attn_kernel, bq=bq, bk=bk, heads=heads),
        out_shape=jax.ShapeDtypeStruct((s, SB_HEADS * SB_HEAD_DIM), BF16),
        grid=(SB_HEADS // heads, s // bq),
        in_specs=[pl.BlockSpec((bq, width), lambda p, i: (i, p)), resident(), resident(), _full((bk, bk))],
        out_specs=pl.BlockSpec((bq, width), lambda p, i: (i, p)),
        scratch_shapes=([pltpu.VMEM((bq, width), F32)] + [pltpu.VMEM((bq, LANES), F32)] * heads
                        + [pltpu.VMEM((bq, bk), F32)] * (2 * heads)),
        compiler_params=_params("arbitrary", "arbitrary"),
        name="sb_attn",
    )(q, k, v, tri)


def _ret_in_kernel(x_ref, mod_ref, ng_ref, cos_ref, sin_ref, w_ref, q_ref, k_ref, v_ref, g_ref, *, k_scale):
    h = _modulated(x_ref[...], ng_ref[0:1, :], mod_ref[0:1, :], mod_ref[1:2, :])
    y = _dot(h, w_ref[...])
    hk, hv = RET_HEADS * RET_KEY_DIM, RET_HEADS * RET_VAL_DIM
    cos, sin = cos_ref[...], sin_ref[...]
    half = RET_KEY_DIM // 2

    def rope(t, scale):
        parts = []
        for hd in range(RET_HEADS):
            t1 = t[:, hd * RET_KEY_DIM:hd * RET_KEY_DIM + half]
            t2 = t[:, hd * RET_KEY_DIM + half:(hd + 1) * RET_KEY_DIM]
            parts.append(((t1 * cos - t2 * sin) * scale).astype(BF16))
            parts.append(((t2 * cos + t1 * sin) * scale).astype(BF16))
        return jnp.concatenate(parts, axis=1)

    q_ref[...] = rope(y[:, :hk], 1.0)
    k_ref[...] = rope(y[:, hk:2 * hk], k_scale)
    v_ref[...] = y[:, 2 * hk:2 * hk + hv].astype(BF16)
    g_ref[...] = y[:, 2 * hk + hv:]


def _ret_in(x, mod, ng, cos, sin, w_in):
    s, d = x.shape
    tm = ROW_TILE
    hk, hv = RET_HEADS * RET_KEY_DIM, RET_HEADS * RET_VAL_DIM
    row = lambda w: pl.BlockSpec((tm, w), lambda i: (i, 0))
    w = w_in.astype(BF16)
    return pl.pallas_call(
        functools.partial(_ret_in_kernel, k_scale=RET_KEY_DIM ** -0.5),
        out_shape=[jax.ShapeDtypeStruct((s, hk), BF16), jax.ShapeDtypeStruct((s, hk), BF16),
                   jax.ShapeDtypeStruct((s, hv), BF16), jax.ShapeDtypeStruct((s, hv), F32)],
        grid=(s // tm,),
        in_specs=[row(d), _full(mod.shape), _full(ng.shape), row(LANES), row(LANES), _full(w.shape)],
        out_specs=[row(hk), row(hk), row(hv), row(hv)],
        compiler_params=_params("arbitrary"),
        name="ret_in",
    )(x, mod, ng, cos, sin, w)


def _ret_core_kernel(lg_ref, q_ref, k_ref, v_ref, g_ref, gn_ref, o_ref, state_ref, *, chunk):
    @pl.when(pl.program_id(0) == 0)
    def _():
        state_ref[...] = jnp.zeros(state_ref.shape, F32)

    row = lax.broadcasted_iota(jnp.int32, (chunk, chunk), 0)
    col = lax.broadcasted_iota(jnp.int32, (chunk, chunk), 1)
    diff = (row - col).astype(F32)
    idx = lax.broadcasted_iota(jnp.int32, (chunk, LANES), 0).astype(F32)
    for hd in range(RET_HEADS):
        lg = lg_ref[hd]
        lg_row = lg[0:1, :]
        decay = jnp.where(diff >= 0.0, jnp.exp(jnp.maximum(diff, 0.0) * _tile_lanes(lg_row, chunk)), 0.0)
        xi = jnp.exp((idx + 1.0) * lg_row)
        zeta = jnp.exp((chunk - 1.0 - idx) * lg_row)
        g_chunk = jnp.exp(chunk * lg)
        q = q_ref[:, hd * RET_KEY_DIM:(hd + 1) * RET_KEY_DIM]
        k = k_ref[:, hd * RET_KEY_DIM:(hd + 1) * RET_KEY_DIM]
        v = v_ref[:, hd * RET_VAL_DIM:(hd + 1) * RET_VAL_DIM]
        state = state_ref[hd]
        sc = _dot_nt(q, k) * decay
        inner = _dot(sc.astype(BF16), v)
        cross = _dot(q, state.astype(BF16)) * _tile_lanes(xi, RET_VAL_DIM)
        kz = (k.astype(F32) * _tile_lanes(zeta, RET_KEY_DIM)).astype(BF16)
        upd = lax.dot_general(kz, v, (((0,), (0,)), ((), ())), preferred_element_type=F32)
        state_ref[hd] = state * _tile_lanes(g_chunk[0:1, :], RET_VAL_DIM) + upd
        o = inner + cross
        o = o * lax.rsqrt(jnp.mean(o * o, axis=-1, keepdims=True) + EPS)
        o = o * gn_ref[:, hd * RET_VAL_DIM:(hd + 1) * RET_VAL_DIM]
        g = g_ref[:, hd * RET_VAL_DIM:(hd + 1) * RET_VAL_DIM]
        o_ref[:, hd * RET_VAL_DIM:(hd + 1) * RET_VAL_DIM] = (g * (1.0 / (1.0 + jnp.exp(-g))) * o).astype(o_ref.dtype)


def _ret_core(q, k, v, g, gn_g):
    s = q.shape[0]
    chunk = RET_CHUNK
    hk, hv = RET_HEADS * RET_KEY_DIM, RET_HEADS * RET_VAL_DIM
    log_gamma = jnp.log(1.0 - 2.0 ** (-5.0 - jnp.arange(RET_HEADS, dtype=F32)))
    lg = jnp.broadcast_to(log_gamma[:, None, None], (RET_HEADS, SUBLANES, LANES))
    row = lambda w: pl.BlockSpec((chunk, w), lambda i: (i, 0))
    return pl.pallas_call(
        functools.partial(_ret_core_kernel, chunk=chunk),
        out_shape=jax.ShapeDtypeStruct((s, hv), BF16),
        grid=(s // chunk,),
        in_specs=[_full(lg.shape), row(hk), row(hk), row(hv), row(hv), _full((1, hv))],
        out_specs=row(hv),
        scratch_shapes=[pltpu.VMEM((RET_HEADS, RET_KEY_DIM, RET_VAL_DIM), F32)],
        compiler_params=_params("arbitrary"),
        name="ret_core",
    )(lg, q, k, v, g, gn_g.reshape(1, hv))


def _post_kernel(o_ref, x_ref, mod_ref, ng_ref, wo_ref, w1_ref, w2_ref, out_ref, y_ref):
    x = x_ref[...]
    y = _dot(o_ref[...], wo_ref[...])
    x = x + mod_ref[2:3, :] * _rms(y, ng_ref[1:2, :])
    h = _modulated(x, ng_ref[2:3, :], mod_ref[3:4, :], mod_ref[4:5, :])
    d_ff = w1_ref.shape[1]
    for c in range(d_ff // FF_CHUNK):
        a = jnp.maximum(_dot(h, w1_ref[:, c * FF_CHUNK:(c + 1) * FF_CHUNK]), 0.0)
        part = _dot((a * a).astype(BF16), w2_ref[c * FF_CHUNK:(c + 1) * FF_CHUNK, :])
        if c == 0:
            y_ref[...] = part
        else:
            y_ref[...] += part
    out_ref[...] = x + mod_ref[5:6, :] * _rms(y_ref[...], ng_ref[3:4, :])


def _post(o, x, mod, ng, w_out, w1, w2):
    s, d = x.shape
    tm = ROW_TILE
    ko = o.shape[1]
    row = lambda w: pl.BlockSpec((tm, w), lambda i: (i, 0))
    resident = lambda a: pl.BlockSpec(a.shape, lambda i: (0, 0), pipeline_mode=pl.Buffered(1))
    wo, w1b, w2b = w_out.astype(BF16), w1.astype(BF16), w2.astype(BF16)
    return pl.pallas_call(
        _post_kernel,
        out_shape=jax.ShapeDtypeStruct((s, d), F32),
        grid=(s // tm,),
        in_specs=[row(ko), row(d), _full(mod.shape), _full(ng.shape), resident(wo), resident(w1b), resident(w2b)],
        out_specs=row(d),
        scratch_shapes=[pltpu.VMEM((tm, d), F32)],
        compiler_params=_params("arbitrary"),
        name="post",
    )(o, x, mod, ng, wo, w1b, w2b)


def kernel(x, c, positions, ada_w, ada_b, norm_g, ffn_w1, ffn_w2, mla_w_in, mla_q_norm, mla_kv_norm, mla_w_q_up,
           mla_w_kv_up, mla_w_out, sb_w_in, sb_w_out, ret_w_in, ret_gn_g, ret_w_out):
    batch, seq, d = x.shape
    assert batch == 1 and seq % MLA_Q_BLOCK == 0 and seq % ROW_TILE == 0
    depth = ada_w.shape[0]
    mod = _adaln(c, ada_w, ada_b)
    tables = _rope_tables(positions)
    xs = x.reshape(seq, d)
    for i in range(depth):
        kind, j = i % 3, i // 3
        if kind == 0:
            o = _mla_mixer(xs, mod[i], norm_g[i], tables, mla_w_in[j], mla_q_norm[j], mla_kv_norm[j],
                           mla_w_q_up[j], mla_w_kv_up[j])
            w_out = mla_w_out[j]
        elif kind == 1:
            q, k, v = _sb_in(xs, mod[i], norm_g[i], sb_w_in[j])
            o, w_out = _sb_attn(q, k, v), sb_w_out[j]
        else:
            q, k, v, g = _ret_in(xs, mod[i], norm_g[i], tables[2], tables[3], ret_w_in[j])
            o, w_out = _ret_core(q, k, v, g, ret_gn_g[j]), ret_w_out[j]
        xs = _post(o, xs, mod[i], norm_g[i], w_out, ffn_w1[i], ffn_w2[i])
    return xs.reshape(batch, seq, d)
```

```python
import functools
import math

import jax
import jax.numpy as jnp
from jax import lax
from jax.experimental import pallas as pl
from jax.experimental.pallas import tpu as pltpu

EPS = 1e-6
ROPE_BASE = 10000.0
MASK_VALUE = -1e30
LOG2E = math.log2(math.e)

LANES = 128
SUBLANES = 8
VMEM_LIMIT_BYTES = 60000 * 1024

MLA_HEADS = 8
MLA_Q_RANK = 256
MLA_KV_RANK = 128
MLA_NOPE = 128
MLA_ROPE = 64
MLA_V = 128
MLA_QK_PAD = 256

SB_HEADS = 16
SB_HEAD_DIM = 64

RET_HEADS = 4
RET_KEY_DIM = 256
RET_VAL_DIM = 512

ROW_TILE = 512
FF_CHUNK = 1024
MLA_BLOCK = 512
MLA_Q_BLOCK = 2048
MLA_Q_GROUP = 256
SB_Q_BLOCK = 512
SB_K_BLOCK = 256
SB_HEADS_PER_STEP = 4
SB_EXP2_CLAMP = 126.0
SB_EXP2_UNDERFLOW = 160.0
RET_CHUNK = 256

F32 = jnp.float32
BF16 = jnp.bfloat16


def _params(*semantics):
    return pltpu.CompilerParams(dimension_semantics=semantics, vmem_limit_bytes=VMEM_LIMIT_BYTES)


def _full(shape):
    nd = len(shape)
    return pl.BlockSpec(shape, lambda *_: (0,) * nd)


def _rms(x, g):
    return x * lax.rsqrt(jnp.mean(x * x, axis=-1, keepdims=True) + EPS) * g


def _dot(a, b):
    return jnp.dot(a, b, preferred_element_type=F32)


def _dot_nt(a, b):
    return lax.dot_general(a, b, (((1,), (1,)), ((), ())), preferred_element_type=F32)


def _tile_lanes(v, width):
    reps = width // LANES
    return v if reps == 1 else jnp.concatenate([v] * reps, axis=1)


def _adaln_kernel(c_ref, w_ref, b_ref, o_ref):
    c = c_ref[...]
    cond = c * (1.0 / (1.0 + jnp.exp(-c)))
    o_ref[0] = jnp.sum(w_ref[0] * cond, axis=0, keepdims=True) + b_ref[0]


def _adaln(c, ada_w, ada_b):
    depth, d, n = ada_w.shape
    tn = 768
    out = pl.pallas_call(
        _adaln_kernel,
        out_shape=jax.ShapeDtypeStruct((depth, 1, n), F32),
        grid=(depth, n // tn),
        in_specs=[
            _full((d, 1)),
            pl.BlockSpec((1, d, tn), lambda i, j: (i, 0, j)),
            pl.BlockSpec((1, 1, tn), lambda i, j: (i, 0, j)),
        ],
        out_specs=pl.BlockSpec((1, 1, tn), lambda i, j: (i, 0, j)),
        compiler_params=_params("arbitrary", "arbitrary"),
        name="adaln",
    )(c.reshape(d, 1), ada_w, ada_b.reshape(depth, 1, n))
    return out.reshape(depth, 6, d)


def _rope_kernel(pos_ref, f64_ref, f256_ref, a_mask_ref, b_sign_ref, a_ref, b_ref, cos_ref, sin_ref):
    pos = pos_ref[...]
    ang = pos * f64_ref[...]
    a_ref[...] = jnp.cos(ang) * a_mask_ref[...]
    b_ref[...] = jnp.sin(ang) * b_sign_ref[...]
    ang = pos * f256_ref[...]
    cos_ref[...] = jnp.cos(ang)
    sin_ref[...] = jnp.sin(ang)


def _rope_tables(positions):
    s = positions.shape[-1]
    tm = 1024
    pos = positions.astype(F32).reshape(s, 1)
    f64 = ROPE_BASE ** (-jnp.arange(0, MLA_ROPE, 2, dtype=F32) / MLA_ROPE)
    f256 = ROPE_BASE ** (-jnp.arange(0, RET_KEY_DIM, 2, dtype=F32) / RET_KEY_DIM)
    quarter = MLA_ROPE // 2
    ones, zeros = jnp.ones((quarter,), F32), jnp.zeros((quarter,), F32)
    a_mask = jnp.concatenate([ones, ones, zeros, zeros]).reshape(1, LANES)
    b_sign = jnp.concatenate([-ones, ones, zeros, zeros]).reshape(1, LANES)
    row = pl.BlockSpec((tm, LANES), lambda i: (i, 0))
    return pl.pallas_call(
        _rope_kernel,
        out_shape=[jax.ShapeDtypeStruct((s, LANES), F32)] * 4,
        grid=(s // tm,),
        in_specs=[pl.BlockSpec((tm, 1), lambda i: (i, 0))] + [_full((1, LANES))] * 4,
        out_specs=[row] * 4,
        compiler_params=_params("arbitrary"),
        name="rope_tables",
    )(pos, jnp.tile(f64, 4).reshape(1, LANES), f256.reshape(1, LANES), a_mask, b_sign)


def _modulated(x, g, shift, scale):
    return (_rms(x, g) * (1.0 + scale) + shift).astype(BF16)


def _mla_in_kernel(x_ref, mod_ref, ng_ref, a_ref, b_ref, w_in_ref, qn_ref, kvn_ref, wq_ref, wk_ref, wv_ref,
                   q_ref, k_ref, v_ref, *, q_scale):
    h = _modulated(x_ref[...], ng_ref[0:1, :], mod_ref[0:1, :], mod_ref[1:2, :])
    lat = _dot(h, w_in_ref[...])
    rope_a, rope_b = a_ref[...], b_ref[...]

    def rope(v):
        return v * rope_a + pltpu.roll(v, 2 * (MLA_ROPE // 2), axis=1) * rope_b

    q_lat = lat[:, :MLA_Q_RANK]
    kv_lat = lat[:, MLA_Q_RANK:MLA_Q_RANK + MLA_KV_RANK]
    k_pe = rope(lat[:, MLA_Q_RANK + MLA_KV_RANK:]).astype(BF16)
    q = _dot(_rms(q_lat, qn_ref[...]).astype(BF16), wq_ref[...])
    kvn = _rms(kv_lat, kvn_ref[...]).astype(BF16)
    k_nope = _dot(kvn, wk_ref[...]).astype(BF16)
    v_t = _dot_nt(wv_ref[...], kvn).astype(BF16)
    v_ref[...] = v_t.reshape(MLA_HEADS, 1, MLA_V, v_t.shape[1])
    q_parts, k_parts = [], []
    for hd in range(MLA_HEADS):
        lo = hd * MLA_QK_PAD
        q_parts.append((q[:, lo:lo + MLA_NOPE] * q_scale).astype(BF16))
        q_parts.append((rope(q[:, lo + MLA_NOPE:lo + MLA_QK_PAD]) * q_scale).astype(BF16))
        k_parts.append(k_nope[:, hd * MLA_NOPE:(hd + 1) * MLA_NOPE])
        k_parts.append(k_pe)
    q_ref[...] = jnp.concatenate(q_parts, axis=1)
    k_ref[...] = jnp.concatenate(k_parts, axis=1)


def _mla_weights(w_in, w_q_up, w_kv_up):
    half = MLA_ROPE // 2
    pe = w_in[:, MLA_Q_RANK + MLA_KV_RANK:]
    pe1, pe2 = pe[:, :half], pe[:, half:]
    w_in_p = jnp.concatenate([w_in[:, :MLA_Q_RANK + MLA_KV_RANK], pe1, pe2, pe2, pe1], axis=1)
    wq = w_q_up.reshape(MLA_Q_RANK, MLA_HEADS, MLA_NOPE + MLA_ROPE)
    q1, q2 = wq[..., MLA_NOPE:MLA_NOPE + half], wq[..., MLA_NOPE + half:]
    wq_p = jnp.concatenate([wq[..., :MLA_NOPE], q1, q2, q2, q1], axis=-1).reshape(MLA_Q_RANK, MLA_HEADS * MLA_QK_PAD)
    wkv = w_kv_up.reshape(MLA_KV_RANK, MLA_HEADS, MLA_NOPE + MLA_V)
    wk = wkv[..., :MLA_NOPE].reshape(MLA_KV_RANK, MLA_HEADS * MLA_NOPE)
    wv_t = wkv[..., MLA_NOPE:].reshape(MLA_KV_RANK, MLA_HEADS * MLA_V).T
    return w_in_p.astype(BF16), wq_p.astype(BF16), wk.astype(BF16), wv_t.astype(BF16)


def _mla_in(x, mod, ng, rope_a, rope_b, w_in, q_norm, kv_norm, w_q_up, w_kv_up):
    s, d = x.shape
    tm = MLA_BLOCK
    w_in_p, wq_p, wk, wv_t = _mla_weights(w_in, w_q_up, w_kv_up)
    q_scale = (MLA_NOPE + MLA_ROPE) ** -0.5 * LOG2E
    row = lambda n: pl.BlockSpec((tm, n), lambda i: (i, 0))
    hq = MLA_HEADS * MLA_QK_PAD
    return pl.pallas_call(
        functools.partial(_mla_in_kernel, q_scale=q_scale),
        out_shape=[jax.ShapeDtypeStruct((s, hq), BF16), jax.ShapeDtypeStruct((s, hq), BF16),
                   jax.ShapeDtypeStruct((MLA_HEADS, s // tm, MLA_V, tm), BF16)],
        grid=(s // tm,),
        in_specs=[row(d), _full(mod.shape), _full(ng.shape), row(LANES), row(LANES), _full(w_in_p.shape),
                  _full((1, MLA_Q_RANK)), _full((1, MLA_KV_RANK)), _full(wq_p.shape), _full(wk.shape),
                  _full(wv_t.shape)],
        out_specs=[row(hq), row(hq), pl.BlockSpec((MLA_HEADS, 1, MLA_V, tm), lambda i: (0, i, 0, 0))],
        compiler_params=_params("arbitrary"),
        name="mla_in",
    )(x, mod, ng, rope_a, rope_b, w_in_p, q_norm.reshape(1, -1), kv_norm.reshape(1, -1), wq_p, wk, wv_t)


def _interleave(producers, consumers, lead=2):
    order = list(producers[:lead])
    rest = list(producers[lead:])
    for f in consumers:
        order.append(f)
        if rest:
            order.append(rest.pop(0))
    return order + rest


def _mla_attn_kernel(q_ref, k_ref, v_ref, o_ref, *scratch, bq, bk, sub):
    i = pl.program_id(1)
    groups = bq // sub
    ratio = bq // bk
    assert bq == ratio * bk and ratio % 2 == 0
    m_refs, l_refs = scratch[:groups], scratch[groups:2 * groups]
    acc_refs = scratch[2 * groups:3 * groups]
    s_refs = (scratch[3 * groups:4 * groups], scratch[4 * groups:5 * groups])
    qt_ref = scratch[5 * groups]
    qt_ref[...] = q_ref[...].astype(F32).T.astype(BF16)
    for c in range(groups):
        m_refs[c][...] = jnp.full((1, sub), MASK_VALUE, F32)
        l_refs[c][...] = jnp.zeros((1, sub), F32)
        acc_refs[c][...] = jnp.zeros((MLA_V, sub), F32)

    def offset_of(c, diag):
        return None if diag is None else c * sub - diag * bk

    def live_groups(diag):
        return [c for c in range(groups) if diag is None or offset_of(c, diag) >= -(sub - 1)]

    def producers(slot, j, diag):
        start = pl.multiple_of(j * bk, bk)
        k = k_ref[pl.ds(start, bk), :]

        def make(c):
            def produce():
                s_refs[slot][c][...] = _dot(k, qt_ref[:, c * sub:(c + 1) * sub])
            return produce
        return [make(c) for c in live_groups(diag)]

    def consumers(slot, j, diag):
        v = v_ref[0, j]

        def make(c):
            offset = offset_of(c, diag)

            def consume():
                s = s_refs[slot][c][...]
                if offset is not None and offset < bk - 1:
                    key = lax.broadcasted_iota(jnp.int32, (bk, sub), 0)
                    qry = lax.broadcasted_iota(jnp.int32, (bk, sub), 1)
                    s = jnp.where(key - qry <= offset, s, MASK_VALUE)
                m_prev = m_refs[c][...]
                m_new = jnp.maximum(m_prev, jnp.max(s, axis=0, keepdims=True))
                alpha = jnp.exp2(m_prev - m_new)
                p = jnp.exp2(s - m_new)
                l_refs[c][...] = alpha * l_refs[c][...] + jnp.sum(p, axis=0, keepdims=True)
                m_refs[c][...] = m_new
                acc_refs[c][...] = alpha * acc_refs[c][...] + _dot(v, p.astype(BF16))
            return consume
        return [make(c) for c in live_groups(diag)]

    def stage(slot, j, diag, next_j, next_diag):
        nxt = [] if next_j is None else producers(1 - slot, next_j, next_diag)
        for f in _interleave(nxt, consumers(slot, j, diag)):
            f()

    for f in producers(0, 0, None):
        f()

    def body(t, carry):
        stage(0, 2 * t, None, 2 * t + 1, None)
        stage(1, 2 * t + 1, None, 2 * t + 2, None)
        return carry

    first_diag = i * ratio
    lax.fori_loop(0, first_diag // 2, body, 0)
    for dj in range(ratio):
        last = dj == ratio - 1
        stage(dj % 2, first_diag + dj, dj, None if last else first_diag + dj + 1, None if last else dj + 1)
    for c in range(groups):
        o_ref[c * sub:(c + 1) * sub, :] = (acc_refs[c][...] / l_refs[c][...]).T.astype(o_ref.dtype)


def _mla_attn(q, k, v_t):
    s = q.shape[0]
    bq, bk, sub = MLA_Q_BLOCK, MLA_BLOCK, MLA_Q_GROUP
    groups = bq // sub
    return pl.pallas_call(
        functools.partial(_mla_attn_kernel, bq=bq, bk=bk, sub=sub),
        out_shape=jax.ShapeDtypeStruct((s, MLA_HEADS * MLA_V), BF16),
        grid=(MLA_HEADS, s // bq),
        in_specs=[pl.BlockSpec((bq, MLA_QK_PAD), lambda h, i: (i, h)),
                  pl.BlockSpec((s, MLA_QK_PAD), lambda h, i: (0, h)),
                  pl.BlockSpec((1, s // bk, MLA_V, bk), lambda h, i: (h, 0, 0, 0))],
        out_specs=pl.BlockSpec((bq, MLA_V), lambda h, i: (i, h)),
        scratch_shapes=([pltpu.VMEM((1, sub), F32)] * (2 * groups) + [pltpu.VMEM((MLA_V, sub), F32)] * groups
                        + [pltpu.VMEM((bk, sub), F32)] * (2 * groups) + [pltpu.VMEM((MLA_QK_PAD, bq), BF16)]),
        compiler_params=_params("arbitrary", "arbitrary"),
        name="mla_attn",
    )(q, k, v_t)


def _mla_mixer(x, mod, ng, tables, w_in, q_norm, kv_norm, w_q_up, w_kv_up):
    q, k, v = _mla_in(x, mod, ng, tables[0], tables[1], w_in, q_norm, kv_norm, w_q_up, w_kv_up)
    return _mla_attn(q, k, v)


def _sb_in_kernel(x_ref, mod_ref, ng_ref, w_ref, q_ref, k_ref, v_ref, *, q_scale):
    h = _modulated(x_ref[...], ng_ref[0:1, :], mod_ref[0:1, :], mod_ref[1:2, :])
    qkv = _dot(h, w_ref[...])
    n = SB_HEADS * SB_HEAD_DIM
    q_ref[...] = (qkv[:, :n] * q_scale).astype(BF16)
    k_ref[...] = qkv[:, n:2 * n].astype(BF16)
    v_ref[...] = qkv[:, 2 * n:].astype(BF16)


def _sb_in(x, mod, ng, w_in):
    s, d = x.shape
    tm = ROW_TILE
    n = SB_HEADS * SB_HEAD_DIM
    row = lambda w: pl.BlockSpec((tm, w), lambda i: (i, 0))
    w = w_in.astype(BF16)
    return pl.pallas_call(
        functools.partial(_sb_in_kernel, q_scale=-LOG2E * SB_HEAD_DIM ** -0.5),
        out_shape=[jax.ShapeDtypeStruct((s, n), BF16)] * 3,
        grid=(s // tm,),
        in_specs=[row(d), _full(mod.shape), _full(ng.shape), _full(w.shape)],
        out_specs=[row(n)] * 3,
        compiler_params=_params("arbitrary"),
        name="sb_in",
    )(x, mod, ng, w)


def _sb_attn_kernel(q_ref, k_ref, v_ref, tri_ref, o_ref, acc_ref, *scratch, bq, bk, heads):
    i = pl.program_id(1)
    assert bq == 2 * bk
    width = heads * SB_HEAD_DIM
    c_refs = scratch[:heads]
    w_refs = (scratch[heads:2 * heads], scratch[2 * heads:3 * heads])
    q = q_ref[...]
    lane_q = lax.broadcasted_iota(jnp.int32, q.shape, 1)
    lane_v = lax.broadcasted_iota(jnp.int32, (bk, width), 1)

    def head_lanes(lane, hd):
        return (lane >= hd * SB_HEAD_DIM) & (lane < (hd + 1) * SB_HEAD_DIM)

    q_heads = [jnp.where(head_lanes(lane_q, hd), q, jnp.zeros_like(q)) for hd in range(heads)]
    acc_ref[...] = jnp.zeros(acc_ref.shape, F32)
    for hd in range(heads):
        c_refs[hd][...] = jnp.zeros((bq, LANES), F32)
    tri = tri_ref[...]

    def producers(slot, j):
        start = pl.multiple_of(j * bk, bk)
        k = k_ref[pl.ds(start, bk), :]

        def make(hd):
            def produce():
                w_refs[slot][hd][...] = _dot_nt(q_heads[hd], k)
            return produce
        return [make(hd) for hd in range(heads)]

    def stage(slot, j, mask_offset, next_j):
        start = pl.multiple_of(j * bk, bk)
        v = v_ref[pl.ds(start, bk), :]
        if mask_offset is not None:
            row = lax.broadcasted_iota(jnp.int32, (bq, bk), 0)
            col = lax.broadcasted_iota(jnp.int32, (bq, bk), 1)
            strict = (col - row) < mask_offset
        weights = [None] * heads

        def make(hd):
            def consume():
                w = w_refs[slot][hd][...]
                neg_log_b = jnp.log2(1.0 + jnp.exp2(jnp.minimum(w, SB_EXP2_CLAMP)))
                log_1m = jnp.minimum(w - neg_log_b, 0.0)
                if mask_offset is not None:
                    log_1m = jnp.where(strict, log_1m, 0.0)
                suffix = _dot(log_1m.astype(BF16), tri)
                c = c_refs[hd][...]
                a = jnp.exp2(suffix - neg_log_b + _tile_lanes(c, bk))
                if mask_offset is not None:
                    a = jnp.where(strict, a, 0.0)
                weights[hd] = a.astype(BF16)
                c_refs[hd][...] = c + jnp.sum(log_1m, axis=1, keepdims=True)
            return consume

        nxt = [] if next_j is None else producers(1 - slot, next_j)
        for f in _interleave(nxt, [make(hd) for hd in range(heads)]):
            f()
        v_heads = [jnp.where(head_lanes(lane_v, hd), v, jnp.zeros_like(v)) for hd in range(heads)]
        acc_ref[...] += _dot(jnp.concatenate(weights, axis=1), jnp.concatenate(v_heads, axis=0))

    for f in producers(0, 2 * i + 1):
        f()
    stage(0, 2 * i + 1, -bk, 2 * i)
    stage(1, 2 * i, 0, jnp.maximum(2 * i - 1, 0))

    def any_weight_left():
        worst = c_refs[0][...]
        for hd in range(1, heads):
            worst = jnp.maximum(worst, c_refs[hd][...])
        return (jnp.max(worst) > -SB_EXP2_UNDERFLOW).astype(jnp.int32)

    def more(carry):
        t, left = carry
        return jnp.logical_and(t < i, left > 0)

    def body(carry):
        t, _ = carry
        j = 2 * i - 1 - 2 * t
        stage(0, j, None, j - 1)
        stage(1, j - 1, None, jnp.maximum(j - 2, 0))
        return t + 1, any_weight_left()

    lax.while_loop(more, body, (0, any_weight_left()))
    o_ref[...] = acc_ref[...].astype(o_ref.dtype)


def _sb_attn(q, k, v):
    s = q.shape[0]
    bq, bk, heads = SB_Q_BLOCK, SB_K_BLOCK, SB_HEADS_PER_STEP
    width = heads * SB_HEAD_DIM
    tri = jnp.tril(jnp.ones((bk, bk), F32), k=-1).astype(BF16)
    resident = lambda: pl.BlockSpec((s, width), lambda p, i: (0, p), pipeline_mode=pl.Buffered(1))
    return pl.pallas_call(
        functools.partial(_sb_attn_kernel, bq=bq, bk=bk, heads=heads),
        out_shape=jax.ShapeDtypeStruct((s, SB_HEADS * SB_HEAD_DIM), BF16),
        grid=(SB_HEADS // heads, s // bq),
        in_specs=[pl.BlockSpec((bq, width), lambda p, i: (i, p)), resident(), resident(), _full((bk, bk))],
        out_specs=pl.BlockSpec((bq, width), lambda p, i: (i, p)),
        scratch_shapes=([pltpu.VMEM((bq, width), F32)] + [pltpu.VMEM((bq, LANES), F32)] * heads
                        + [pltpu.VMEM((bq, bk), F32)] * (2 * heads)),
        compiler_params=_params("arbitrary", "arbitrary"),
        name="sb_attn",
    )(q, k, v, tri)


def _ret_in_kernel(x_ref, mod_ref, ng_ref, cos_ref, sin_ref, w_ref, q_ref, k_ref, v_ref, g_ref, *, k_scale):
    h = _modulated(x_ref[...], ng_ref[0:1, :], mod_ref[0:1, :], mod_ref[1:2, :])
    y = _dot(h, w_ref[...])
    hk, hv = RET_HEADS * RET_KEY_DIM, RET_HEADS * RET_VAL_DIM
    cos, sin = cos_ref[...], sin_ref[...]
    half = RET_KEY_DIM // 2

    def rope(t, scale):
        parts = []
        for hd in range(RET_HEADS):
            t1 = t[:, hd * RET_KEY_DIM:hd * RET_KEY_DIM + half]
            t2 = t[:, hd * RET_KEY_DIM + half:(hd + 1) * RET_KEY_DIM]
            parts.append(((t1 * cos - t2 * sin) * scale).astype(BF16))
            parts.append(((t2 * cos + t1 * sin) * scale).astype(BF16))
        return jnp.concatenate(parts, axis=1)

    q_ref[...] = rope(y[:, :hk], 1.0)
    k_ref[...] = rope(y[:, hk:2 * hk], k_scale)
    v_ref[...] = y[:, 2 * hk:2 * hk + hv].astype(BF16)
    g_ref[...] = y[:, 2 * hk + hv:]


def _ret_in(x, mod, ng, cos, sin, w_in):
    s, d = x.shape
    tm = ROW_TILE
    hk, hv = RET_HEADS * RET_KEY_DIM, RET_HEADS * RET_VAL_DIM
    row = lambda w: pl.BlockSpec((tm, w), lambda i: (i, 0))
    w = w_in.astype(BF16)
    return pl.pallas_call(
        functools.partial(_ret_in_kernel, k_scale=RET_KEY_DIM ** -0.5),
        out_shape=[jax.ShapeDtypeStruct((s, hk), BF16), jax.ShapeDtypeStruct((s, hk), BF16),
                   jax.ShapeDtypeStruct((s, hv), BF16), jax.ShapeDtypeStruct((s, hv), F32)],
        grid=(s // tm,),
        in_specs=[row(d), _full(mod.shape), _full(ng.shape), row(LANES), row(LANES), _full(w.shape)],
        out_specs=[row(hk), row(hk), row(hv), row(hv)],
        compiler_params=_params("arbitrary"),
        name="ret_in",
    )(x, mod, ng, cos, sin, w)


def _ret_core_kernel(lg_ref, q_ref, k_ref, v_ref, g_ref, gn_ref, o_ref, state_ref, *, chunk):
    @pl.when(pl.program_id(0) == 0)
    def _():
        state_ref[...] = jnp.zeros(state_ref.shape, F32)

    row = lax.broadcasted_iota(jnp.int32, (chunk, chunk), 0)
    col = lax.broadcasted_iota(jnp.int32, (chunk, chunk), 1)
    diff = (row - col).astype(F32)
    idx = lax.broadcasted_iota(jnp.int32, (chunk, LANES), 0).astype(F32)
    for hd in range(RET_HEADS):
        lg = lg_ref[hd]
        lg_row = lg[0:1, :]
        decay = jnp.where(diff >= 0.0, jnp.exp(jnp.maximum(diff, 0.0) * _tile_lanes(lg_row, chunk)), 0.0)
        xi = jnp.exp((idx + 1.0) * lg_row)
        zeta = jnp.exp((chunk - 1.0 - idx) * lg_row)
        g_chunk = jnp.exp(chunk * lg)
        q = q_ref[:, hd * RET_KEY_DIM:(hd + 1) * RET_KEY_DIM]
        k = k_ref[:, hd * RET_KEY_DIM:(hd + 1) * RET_KEY_DIM]
        v = v_ref[:, hd * RET_VAL_DIM:(hd + 1) * RET_VAL_DIM]
        state = state_ref[hd]
        sc = _dot_nt(q, k) * decay
        inner = _dot(sc.astype(BF16), v)
        cross = _dot(q, state.astype(BF16)) * _tile_lanes(xi, RET_VAL_DIM)
        kz = (k.astype(F32) * _tile_lanes(zeta, RET_KEY_DIM)).astype(BF16)
        upd = lax.dot_general(kz, v, (((0,), (0,)), ((), ())), preferred_element_type=F32)
        state_ref[hd] = state * _tile_lanes(g_chunk[0:1, :], RET_VAL_DIM) + upd
        o = inner + cross
        o = o * lax.rsqrt(jnp.mean(o * o, axis=-1, keepdims=True) + EPS)
        o = o * gn_ref[:, hd * RET_VAL_DIM:(hd + 1) * RET_VAL_DIM]
        g = g_ref[:, hd * RET_VAL_DIM:(hd + 1) * RET_VAL_DIM]
        o_ref[:, hd * RET_VAL_DIM:(hd + 1) * RET_VAL_DIM] = (g * (1.0 / (1.0 + jnp.exp(-g))) * o).astype(o_ref.dtype)


def _ret_core(q, k, v, g, gn_g):
    s = q.shape[0]
    chunk = RET_CHUNK
    hk, hv = RET_HEADS * RET_KEY_DIM, RET_HEADS * RET_VAL_DIM
    log_gamma = jnp.log(1.0 - 2.0 ** (-5.0 - jnp.arange(RET_HEADS, dtype=F32)))
    lg = jnp.broadcast_to(log_gamma[:, None, None], (RET_HEADS, SUBLANES, LANES))
    row = lambda w: pl.BlockSpec((chunk, w), lambda i: (i, 0))
    return pl.pallas_call(
        functools.partial(_ret_core_kernel, chunk=chunk),
        out_shape=jax.ShapeDtypeStruct((s, hv), BF16),
        grid=(s // chunk,),
        in_specs=[_full(lg.shape), row(hk), row(hk), row(hv), row(hv), _full((1, hv))],
        out_specs=row(hv),
        scratch_shapes=[pltpu.VMEM((RET_HEADS, RET_KEY_DIM, RET_VAL_DIM), F32)],
        compiler_params=_params("arbitrary"),
        name="ret_core",
    )(lg, q, k, v, g, gn_g.reshape(1, hv))


def _post_kernel(o_ref, x_ref, mod_ref, ng_ref, wo_ref, w1_ref, w2_ref, out_ref, y_ref):
    x = x_ref[...]
    y = _dot(o_ref[...], wo_ref[...])
    x = x + mod_ref[2:3, :] * _rms(y, ng_ref[1:2, :])
    h = _modulated(x, ng_ref[2:3, :], mod_ref[3:4, :], mod_ref[4:5, :])
    d_ff = w1_ref.shape[1]
    for c in range(d_ff // FF_CHUNK):
        a = jnp.maximum(_dot(h, w1_ref[:, c * FF_CHUNK:(c + 1) * FF_CHUNK]), 0.0)
        part = _dot((a * a).astype(BF16), w2_ref[c * FF_CHUNK:(c + 1) * FF_CHUNK, :])
        if c == 0:
            y_ref[...] = part
        else:
            y_ref[...] += part
    out_ref[...] = x + mod_ref[5:6, :] * _rms(y_ref[...], ng_ref[3:4, :])


def _post(o, x, mod, ng, w_out, w1_all, w2_all, layer):
    s, d = x.shape
    tm = ROW_TILE
    ko = o.shape[1]
    row = lambda w: pl.BlockSpec((tm, w), lambda i: (i, 0))
    resident = lambda a: pl.BlockSpec(a.shape, lambda i: (0, 0), pipeline_mode=pl.Buffered(1))
    of_layer = lambda a: pl.BlockSpec((None,) + a.shape[1:], lambda i: (layer, 0, 0), pipeline_mode=pl.Buffered(1))
    wo = w_out.astype(BF16)
    return pl.pallas_call(
        _post_kernel,
        out_shape=jax.ShapeDtypeStruct((s, d), F32),
        grid=(s // tm,),
        in_specs=[row(ko), row(d), _full(mod.shape), _full(ng.shape), resident(wo), of_layer(w1_all),
                  of_layer(w2_all)],
        out_specs=row(d),
        scratch_shapes=[pltpu.VMEM((tm, d), F32)],
        compiler_params=_params("arbitrary"),
        name="post",
    )(o, x, mod, ng, wo, w1_all, w2_all)


def kernel(x, c, positions, ada_w, ada_b, norm_g, ffn_w1, ffn_w2, mla_w_in, mla_q_norm, mla_kv_norm, mla_w_q_up,
           mla_w_kv_up, mla_w_out, sb_w_in, sb_w_out, ret_w_in, ret_gn_g, ret_w_out):
    batch, seq, d = x.shape
    assert batch == 1 and seq % MLA_Q_BLOCK == 0 and seq % ROW_TILE == 0
    depth = ada_w.shape[0]
    mod = _adaln(c, ada_w, ada_b)
    tables = _rope_tables(positions)
    w1_all, w2_all = ffn_w1.astype(BF16), ffn_w2.astype(BF16)
    xs = x.reshape(seq, d)
    for i in range(depth):
        kind, j = i % 3, i // 3
        if kind == 0:
            o = _mla_mixer(xs, mod[i], norm_g[i], tables, mla_w_in[j], mla_q_norm[j], mla_kv_norm[j],
                           mla_w_q_up[j], mla_w_kv_up[j])
            w_out = mla_w_out[j]
        elif kind == 1:
            q, k, v = _sb_in(xs, mod[i], norm_g[i], sb_w_in[j])
            o, w_out = _sb_attn(q, k, v), sb_w_out[j]
        else:
            q, k, v, g = _ret_in(xs, mod[i], norm_g[i], tables[2], tables[3], ret_w_in[j])
            o, w_out = _ret_core(q, k, v, g, ret_gn_g[j]), ret_w_out[j]
        xs = _post(o, xs, mod[i], norm_g[i], w_out, w1_all, w2_all, i)
    return xs.reshape(batch, seq, d)
```

```python
import functools
import math

import jax
import jax.numpy as jnp
from jax import lax
from jax.experimental import pallas as pl
from jax.experimental.pallas import tpu as pltpu

EPS = 1e-6
ROPE_BASE = 10000.0
MASK_VALUE = -1e30
LOG2E = math.log2(math.e)

LANES = 128
SUBLANES = 8
VMEM_LIMIT_BYTES = 60000 * 1024

MLA_HEADS = 8
MLA_Q_RANK = 256
MLA_KV_RANK = 128
MLA_NOPE = 128
MLA_ROPE = 64
MLA_V = 128
MLA_QK_PAD = 256

SB_HEADS = 16
SB_HEAD_DIM = 64

RET_HEADS = 4
RET_KEY_DIM = 256
RET_VAL_DIM = 512

ROW_TILE = 512
FF_CHUNK = 1024
MLA_BLOCK = 512
MLA_Q_BLOCK = 2048
MLA_Q_GROUP = 256
SB_Q_BLOCK = 512
SB_K_BLOCK = 256
SB_HEADS_PER_STEP = 4
SB_EXP2_CLAMP = 126.0
SB_EXP2_UNDERFLOW = 160.0
RET_CHUNK = 256

F32 = jnp.float32
BF16 = jnp.bfloat16


def _params(*semantics):
    return pltpu.CompilerParams(dimension_semantics=semantics, vmem_limit_bytes=VMEM_LIMIT_BYTES)


def _full(shape):
    nd = len(shape)
    return pl.BlockSpec(shape, lambda *_: (0,) * nd)


def _rms(x, g):
    return x * lax.rsqrt(jnp.mean(x * x, axis=-1, keepdims=True) + EPS) * g


def _dot(a, b):
    return jnp.dot(a, b, preferred_element_type=F32)


def _dot_nt(a, b):
    return lax.dot_general(a, b, (((1,), (1,)), ((), ())), preferred_element_type=F32)


def _tile_lanes(v, width):
    reps = width // LANES
    return v if reps == 1 else jnp.concatenate([v] * reps, axis=1)


def _adaln_kernel(c_ref, w_ref, b_ref, o_ref):
    c = c_ref[...]
    cond = c * (1.0 / (1.0 + jnp.exp(-c)))
    o_ref[0] = jnp.sum(w_ref[0] * cond, axis=0, keepdims=True) + b_ref[0]


def _adaln(c, ada_w, ada_b):
    depth, d, n = ada_w.shape
    tn = 768
    out = pl.pallas_call(
        _adaln_kernel,
        out_shape=jax.ShapeDtypeStruct((depth, 1, n), F32),
        grid=(depth, n // tn),
        in_specs=[
            _full((d, 1)),
            pl.BlockSpec((1, d, tn), lambda i, j: (i, 0, j)),
            pl.BlockSpec((1, 1, tn), lambda i, j: (i, 0, j)),
        ],
        out_specs=pl.BlockSpec((1, 1, tn), lambda i, j: (i, 0, j)),
        compiler_params=_params("arbitrary", "arbitrary"),
        name="adaln",
    )(c.reshape(d, 1), ada_w, ada_b.reshape(depth, 1, n))
    return out.reshape(depth, 6, d)


def _rope_kernel(pos_ref, f64_ref, f256_ref, a_mask_ref, b_sign_ref, a_ref, b_ref, cos_ref, sin_ref):
    pos = pos_ref[...]
    ang = pos * f64_ref[...]
    a_ref[...] = jnp.cos(ang) * a_mask_ref[...]
    b_ref[...] = jnp.sin(ang) * b_sign_ref[...]
    ang = pos * f256_ref[...]
    cos_ref[...] = jnp.cos(ang)
    sin_ref[...] = jnp.sin(ang)


def _rope_tables(positions):
    s = positions.shape[-1]
    tm = 1024
    pos = positions.astype(F32).reshape(s, 1)
    f64 = ROPE_BASE ** (-jnp.arange(0, MLA_ROPE, 2, dtype=F32) / MLA_ROPE)
    f256 = ROPE_BASE ** (-jnp.arange(0, RET_KEY_DIM, 2, dtype=F32) / RET_KEY_DIM)
    quarter = MLA_ROPE // 2
    ones, zeros = jnp.ones((quarter,), F32), jnp.zeros((quarter,), F32)
    a_mask = jnp.concatenate([ones, ones, zeros, zeros]).reshape(1, LANES)
    b_sign = jnp.concatenate([-ones, ones, zeros, zeros]).reshape(1, LANES)
    row = pl.BlockSpec((tm, LANES), lambda i: (i, 0))
    return pl.pallas_call(
        _rope_kernel,
        out_shape=[jax.ShapeDtypeStruct((s, LANES), F32)] * 4,
        grid=(s // tm,),
        in_specs=[pl.BlockSpec((tm, 1), lambda i: (i, 0))] + [_full((1, LANES))] * 4,
        out_specs=[row] * 4,
        compiler_params=_params("arbitrary"),
        name="rope_tables",
    )(pos, jnp.tile(f64, 4).reshape(1, LANES), f256.reshape(1, LANES), a_mask, b_sign)


def _modulated(x, g, shift, scale):
    return (_rms(x, g) * (1.0 + scale) + shift).astype(BF16)


def _mla_in_kernel(x_ref, mod_ref, ng_ref, a_ref, b_ref, w_in_ref, qn_ref, kvn_ref, wq_ref, wk_ref, wv_ref,
                   q_ref, k_ref, v_ref, *, q_scale):
    h = _modulated(x_ref[...], ng_ref[0:1, :], mod_ref[0:1, :], mod_ref[1:2, :])
    lat = _dot(h, w_in_ref[...])
    rope_a, rope_b = a_ref[...], b_ref[...]

    def rope(v):
        return v * rope_a + pltpu.roll(v, 2 * (MLA_ROPE // 2), axis=1) * rope_b

    q_lat = lat[:, :MLA_Q_RANK]
    kv_lat = lat[:, MLA_Q_RANK:MLA_Q_RANK + MLA_KV_RANK]
    k_pe = rope(lat[:, MLA_Q_RANK + MLA_KV_RANK:]).astype(BF16)
    q = _dot(_rms(q_lat, qn_ref[...]).astype(BF16), wq_ref[...])
    kvn = _rms(kv_lat, kvn_ref[...]).astype(BF16)
    k_nope = _dot(kvn, wk_ref[...]).astype(BF16)
    v_t = _dot_nt(wv_ref[...], kvn).astype(BF16)
    v_ref[...] = v_t.reshape(MLA_HEADS, 1, MLA_V, v_t.shape[1])
    q_parts, k_parts = [], []
    for hd in range(MLA_HEADS):
        lo = hd * MLA_QK_PAD
        q_parts.append((q[:, lo:lo + MLA_NOPE] * q_scale).astype(BF16))
        q_parts.append((rope(q[:, lo + MLA_NOPE:lo + MLA_QK_PAD]) * q_scale).astype(BF16))
        k_parts.append(k_nope[:, hd * MLA_NOPE:(hd + 1) * MLA_NOPE])
        k_parts.append(k_pe)
    q_ref[...] = jnp.concatenate(q_parts, axis=1)
    k_ref[...] = jnp.concatenate(k_parts, axis=1)


def _mla_weights(w_in, w_q_up, w_kv_up):
    half = MLA_ROPE // 2
    pe = w_in[:, MLA_Q_RANK + MLA_KV_RANK:]
    pe1, pe2 = pe[:, :half], pe[:, half:]
    w_in_p = jnp.concatenate([w_in[:, :MLA_Q_RANK + MLA_KV_RANK], pe1, pe2, pe2, pe1], axis=1)
    wq = w_q_up.reshape(MLA_Q_RANK, MLA_HEADS, MLA_NOPE + MLA_ROPE)
    q1, q2 = wq[..., MLA_NOPE:MLA_NOPE + half], wq[..., MLA_NOPE + half:]
    wq_p = jnp.concatenate([wq[..., :MLA_NOPE], q1, q2, q2, q1], axis=-1).reshape(MLA_Q_RANK, MLA_HEADS * MLA_QK_PAD)
    wkv = w_kv_up.reshape(MLA_KV_RANK, MLA_HEADS, MLA_NOPE + MLA_V)
    wk = wkv[..., :MLA_NOPE].reshape(MLA_KV_RANK, MLA_HEADS * MLA_NOPE)
    wv_t = wkv[..., MLA_NOPE:].reshape(MLA_KV_RANK, MLA_HEADS * MLA_V).T
    return w_in_p.astype(BF16), wq_p.astype(BF16), wk.astype(BF16), wv_t.astype(BF16)


def _mla_in(x, mod, ng, rope_a, rope_b, w_in, q_norm, kv_norm, w_q_up, w_kv_up):
    s, d = x.shape
    tm = MLA_BLOCK
    w_in_p, wq_p, wk, wv_t = _mla_weights(w_in, w_q_up, w_kv_up)
    q_scale = (MLA_NOPE + MLA_ROPE) ** -0.5 * LOG2E
    row = lambda n: pl.BlockSpec((tm, n), lambda i: (i, 0))
    hq = MLA_HEADS * MLA_QK_PAD
    return pl.pallas_call(
        functools.partial(_mla_in_kernel, q_scale=q_scale),
        out_shape=[jax.ShapeDtypeStruct((s, hq), BF16), jax.ShapeDtypeStruct((s, hq), BF16),
                   jax.ShapeDtypeStruct((MLA_HEADS, s // tm, MLA_V, tm), BF16)],
        grid=(s // tm,),
        in_specs=[row(d), _full(mod.shape), _full(ng.shape), row(LANES), row(LANES), _full(w_in_p.shape),
                  _full((1, MLA_Q_RANK)), _full((1, MLA_KV_RANK)), _full(wq_p.shape), _full(wk.shape),
                  _full(wv_t.shape)],
        out_specs=[row(hq), row(hq), pl.BlockSpec((MLA_HEADS, 1, MLA_V, tm), lambda i: (0, i, 0, 0))],
        compiler_params=_params("arbitrary"),
        name="mla_in",
    )(x, mod, ng, rope_a, rope_b, w_in_p, q_norm.reshape(1, -1), kv_norm.reshape(1, -1), wq_p, wk, wv_t)


def _interleave(producers, consumers, lead=2):
    order = list(producers[:lead])
    rest = list(producers[lead:])
    for f in consumers:
        order.append(f)
        if rest:
            order.append(rest.pop(0))
    return order + rest


def _mla_attn_kernel(q_ref, k_ref, v_ref, o_ref, *scratch, bq, bk, sub):
    i = pl.program_id(1)
    groups = bq // sub
    ratio = bq // bk
    assert bq == ratio * bk and ratio % 2 == 0
    m_refs, l_refs = scratch[:groups], scratch[groups:2 * groups]
    acc_refs = scratch[2 * groups:3 * groups]
    s_refs = (scratch[3 * groups:4 * groups], scratch[4 * groups:5 * groups])
    qt_ref = scratch[5 * groups]
    qt_ref[...] = q_ref[...].astype(F32).T.astype(BF16)
    for c in range(groups):
        m_refs[c][...] = jnp.full((1, sub), MASK_VALUE, F32)
        l_refs[c][...] = jnp.zeros((1, sub), F32)
        acc_refs[c][...] = jnp.zeros((MLA_V, sub), F32)

    def offset_of(c, diag):
        return None if diag is None else c * sub - diag * bk

    def live_groups(diag):
        return [c for c in range(groups) if diag is None or offset_of(c, diag) >= -(sub - 1)]

    def producers(slot, j, diag):
        start = pl.multiple_of(j * bk, bk)
        k = k_ref[pl.ds(start, bk), :]

        def make(c):
            def produce():
                s_refs[slot][c][...] = _dot(k, qt_ref[:, c * sub:(c + 1) * sub])
            return produce
        return [make(c) for c in live_groups(diag)]

    def consumers(slot, j, diag):
        v = v_ref[0, j]

        def make(c):
            offset = offset_of(c, diag)

            def consume():
                s = s_refs[slot][c][...]
                if offset is not None and offset < bk - 1:
                    key = lax.broadcasted_iota(jnp.int32, (bk, sub), 0)
                    qry = lax.broadcasted_iota(jnp.int32, (bk, sub), 1)
                    s = jnp.where(key - qry <= offset, s, MASK_VALUE)
                m_prev = m_refs[c][...]
                m_new = jnp.maximum(m_prev, jnp.max(s, axis=0, keepdims=True))
                alpha = jnp.exp2(m_prev - m_new)
                p = jnp.exp2(s - m_new)
                l_refs[c][...] = alpha * l_refs[c][...] + jnp.sum(p, axis=0, keepdims=True)
                m_refs[c][...] = m_new
                acc_refs[c][...] = alpha * acc_refs[c][...] + _dot(v, p.astype(BF16))
            return consume
        return [make(c) for c in live_groups(diag)]

    def stage(slot, j, diag, next_j, next_diag):
        nxt = [] if next_j is None else producers(1 - slot, next_j, next_diag)
        for f in _interleave(nxt, consumers(slot, j, diag)):
            f()

    for f in producers(0, 0, None):
        f()

    def body(t, carry):
        stage(0, 2 * t, None, 2 * t + 1, None)
        stage(1, 2 * t + 1, None, 2 * t + 2, None)
        return carry

    first_diag = i * ratio
    lax.fori_loop(0, first_diag // 2, body, 0)
    for dj in range(ratio):
        last = dj == ratio - 1
        stage(dj % 2, first_diag + dj, dj, None if last else first_diag + dj + 1, None if last else dj + 1)
    for c in range(groups):
        o_ref[c * sub:(c + 1) * sub, :] = (acc_refs[c][...] / l_refs[c][...]).T.astype(o_ref.dtype)


def _mla_attn(q, k, v_t):
    s = q.shape[0]
    bq, bk, sub = MLA_Q_BLOCK, MLA_BLOCK, MLA_Q_GROUP
    groups = bq // sub
    return pl.pallas_call(
        functools.partial(_mla_attn_kernel, bq=bq, bk=bk, sub=sub),
        out_shape=jax.ShapeDtypeStruct((s, MLA_HEADS * MLA_V), BF16),
        grid=(MLA_HEADS, s // bq),
        in_specs=[pl.BlockSpec((bq, MLA_QK_PAD), lambda h, i: (i, h)),
                  pl.BlockSpec((s, MLA_QK_PAD), lambda h, i: (0, h)),
                  pl.BlockSpec((1, s // bk, MLA_V, bk), lambda h, i: (h, 0, 0, 0))],
        out_specs=pl.BlockSpec((bq, MLA_V), lambda h, i: (i, h)),
        scratch_shapes=([pltpu.VMEM((1, sub), F32)] * (2 * groups) + [pltpu.VMEM((MLA_V, sub), F32)] * groups
                        + [pltpu.VMEM((bk, sub), F32)] * (2 * groups) + [pltpu.VMEM((MLA_QK_PAD, bq), BF16)]),
        compiler_params=_params("arbitrary", "arbitrary"),
        name="mla_attn",
    )(q, k, v_t)


def _mla_mixer(x, mod, ng, tables, w_in, q_norm, kv_norm, w_q_up, w_kv_up):
    q, k, v = _mla_in(x, mod, ng, tables[0], tables[1], w_in, q_norm, kv_norm, w_q_up, w_kv_up)
    return _mla_attn(q, k, v)


def _sb_in_kernel(x_ref, mod_ref, ng_ref, wqv_ref, wkt_ref, q_ref, kt_ref, v_ref, *, q_scale, bk):
    h = _modulated(x_ref[...], ng_ref[0:1, :], mod_ref[0:1, :], mod_ref[1:2, :])
    qv = _dot(h, wqv_ref[...])
    n = SB_HEADS * SB_HEAD_DIM
    q_ref[...] = (qv[:, :n] * q_scale).astype(BF16)
    v_ref[...] = qv[:, n:].astype(BF16)
    k_t = _dot_nt(wkt_ref[...], h).astype(BF16)
    for b in range(k_t.shape[1] // bk):
        kt_ref[b] = k_t[:, b * bk:(b + 1) * bk]


def _sb_in(x, mod, ng, w_in):
    s, d = x.shape
    tm, bk = ROW_TILE, SB_K_BLOCK
    n = SB_HEADS * SB_HEAD_DIM
    row = lambda w: pl.BlockSpec((tm, w), lambda i: (i, 0))
    w = w_in.astype(BF16)
    w_qv = jnp.concatenate([w[:, :n], w[:, 2 * n:]], axis=1)
    wk_t = w[:, n:2 * n].T
    return pl.pallas_call(
        functools.partial(_sb_in_kernel, q_scale=-LOG2E * SB_HEAD_DIM ** -0.5, bk=bk),
        out_shape=[jax.ShapeDtypeStruct((s, n), BF16), jax.ShapeDtypeStruct((s // bk, n, bk), BF16),
                   jax.ShapeDtypeStruct((s, n), BF16)],
        grid=(s // tm,),
        in_specs=[row(d), _full(mod.shape), _full(ng.shape), _full(w_qv.shape), _full(wk_t.shape)],
        out_specs=[row(n), pl.BlockSpec((tm // bk, n, bk), lambda i: (i, 0, 0)), row(n)],
        compiler_params=_params("arbitrary"),
        name="sb_in",
    )(x, mod, ng, w_qv, wk_t)


def _sb_attn_kernel(q_ref, kt_ref, v_ref, tri_ref, o_ref, acc_ref, *scratch, bq, bk, heads):
    i = pl.program_id(1)
    assert bq == 2 * bk
    width = heads * SB_HEAD_DIM
    c_refs = scratch[:heads]
    w_refs = (scratch[heads:2 * heads], scratch[2 * heads:3 * heads])
    q = q_ref[...]
    lane_q = lax.broadcasted_iota(jnp.int32, q.shape, 1)
    lane_v = lax.broadcasted_iota(jnp.int32, (bk, width), 1)

    def head_lanes(lane, hd):
        return (lane >= hd * SB_HEAD_DIM) & (lane < (hd + 1) * SB_HEAD_DIM)

    q_heads = [jnp.where(head_lanes(lane_q, hd), q, jnp.zeros_like(q)) for hd in range(heads)]
    acc_ref[...] = jnp.zeros(acc_ref.shape, F32)
    for hd in range(heads):
        c_refs[hd][...] = jnp.zeros((bq, LANES), F32)
    tri = tri_ref[...]

    def producers(slot, j):
        k_t = kt_ref[j]

        def make(hd):
            def produce():
                w_refs[slot][hd][...] = _dot(q_heads[hd], k_t)
            return produce
        return [make(hd) for hd in range(heads)]

    def stage(slot, j, mask_offset, next_j):
        start = pl.multiple_of(j * bk, bk)
        v = v_ref[pl.ds(start, bk), :]
        if mask_offset is not None:
            row = lax.broadcasted_iota(jnp.int32, (bq, bk), 0)
            col = lax.broadcasted_iota(jnp.int32, (bq, bk), 1)
            strict = (col - row) < mask_offset
        weights = [None] * heads

        def make(hd):
            def consume():
                w = w_refs[slot][hd][...]
                neg_log_b = jnp.log2(1.0 + jnp.exp2(jnp.minimum(w, SB_EXP2_CLAMP)))
                log_1m = jnp.minimum(w - neg_log_b, 0.0)
                if mask_offset is not None:
                    log_1m = jnp.where(strict, log_1m, 0.0)
                suffix = _dot(log_1m.astype(BF16), tri)
                c = c_refs[hd][...]
                a = jnp.exp2(suffix - neg_log_b + _tile_lanes(c, bk))
                if mask_offset is not None:
                    a = jnp.where(strict, a, 0.0)
                weights[hd] = a.astype(BF16)
                c_refs[hd][...] = c + jnp.sum(log_1m, axis=1, keepdims=True)
            return consume

        nxt = [] if next_j is None else producers(1 - slot, next_j)
        for f in _interleave(nxt, [make(hd) for hd in range(heads)]):
            f()
        v_heads = [jnp.where(head_lanes(lane_v, hd), v, jnp.zeros_like(v)) for hd in range(heads)]
        acc_ref[...] += _dot(jnp.concatenate(weights, axis=1), jnp.concatenate(v_heads, axis=0))

    for f in producers(0, 2 * i + 1):
        f()
    stage(0, 2 * i + 1, -bk, 2 * i)
    stage(1, 2 * i, 0, jnp.maximum(2 * i - 1, 0))

    def any_weight_left():
        worst = c_refs[0][...]
        for hd in range(1, heads):
            worst = jnp.maximum(worst, c_refs[hd][...])
        return (jnp.max(worst) > -SB_EXP2_UNDERFLOW).astype(jnp.int32)

    def more(carry):
        t, left = carry
        return jnp.logical_and(t < i, left > 0)

    def body(carry):
        t, _ = carry
        j = 2 * i - 1 - 2 * t
        stage(0, j, None, j - 1)
        stage(1, j - 1, None, jnp.maximum(j - 2, 0))
        return t + 1, any_weight_left()

    lax.while_loop(more, body, (0, any_weight_left()))
    o_ref[...] = acc_ref[...].astype(o_ref.dtype)


def _sb_attn(q, k_t, v):
    s = q.shape[0]
    bq, bk, heads = SB_Q_BLOCK, SB_K_BLOCK, SB_HEADS_PER_STEP
    width = heads * SB_HEAD_DIM
    tri = jnp.tril(jnp.ones((bk, bk), F32), k=-1).astype(BF16)
    return pl.pallas_call(
        functools.partial(_sb_attn_kernel, bq=bq, bk=bk, heads=heads),
        out_shape=jax.ShapeDtypeStruct((s, SB_HEADS * SB_HEAD_DIM), BF16),
        grid=(SB_HEADS // heads, s // bq),
        in_specs=[pl.BlockSpec((bq, width), lambda p, i: (i, p)),
                  pl.BlockSpec((s // bk, width, bk), lambda p, i: (0, p, 0), pipeline_mode=pl.Buffered(1)),
                  pl.BlockSpec((s, width), lambda p, i: (0, p), pipeline_mode=pl.Buffered(1)), _full((bk, bk))],
        out_specs=pl.BlockSpec((bq, width), lambda p, i: (i, p)),
        scratch_shapes=([pltpu.VMEM((bq, width), F32)] + [pltpu.VMEM((bq, LANES), F32)] * heads
                        + [pltpu.VMEM((bq, bk), F32)] * (2 * heads)),
        compiler_params=_params("arbitrary", "arbitrary"),
        name="sb_attn",
    )(q, k_t, v, tri)


def _ret_in_kernel(x_ref, mod_ref, ng_ref, cos_ref, sin_ref, w_ref, q_ref, k_ref, v_ref, g_ref, *, k_scale):
    h = _modulated(x_ref[...], ng_ref[0:1, :], mod_ref[0:1, :], mod_ref[1:2, :])
    y = _dot(h, w_ref[...])
    hk, hv = RET_HEADS * RET_KEY_DIM, RET_HEADS * RET_VAL_DIM
    cos, sin = cos_ref[...], sin_ref[...]
    half = RET_KEY_DIM // 2

    def rope(t, scale):
        parts = []
        for hd in range(RET_HEADS):
            t1 = t[:, hd * RET_KEY_DIM:hd * RET_KEY_DIM + half]
            t2 = t[:, hd * RET_KEY_DIM + half:(hd + 1) * RET_KEY_DIM]
            parts.append(((t1 * cos - t2 * sin) * scale).astype(BF16))
            parts.append(((t2 * cos + t1 * sin) * scale).astype(BF16))
        return jnp.concatenate(parts, axis=1)

    q_ref[...] = rope(y[:, :hk], 1.0)
    k_ref[...] = rope(y[:, hk:2 * hk], k_scale)
    v_ref[...] = y[:, 2 * hk:2 * hk + hv].astype(BF16)
    g_ref[...] = y[:, 2 * hk + hv:]


def _ret_in(x, mod, ng, cos, sin, w_in):
    s, d = x.shape
    tm = ROW_TILE
    hk, hv = RET_HEADS * RET_KEY_DIM, RET_HEADS * RET_VAL_DIM
    row = lambda w: pl.BlockSpec((tm, w), lambda i: (i, 0))
    w = w_in.astype(BF16)
    return pl.pallas_call(
        functools.partial(_ret_in_kernel, k_scale=RET_KEY_DIM ** -0.5),
        out_shape=[jax.ShapeDtypeStruct((s, hk), BF16), jax.ShapeDtypeStruct((s, hk), BF16),
                   jax.ShapeDtypeStruct((s, hv), BF16), jax.ShapeDtypeStruct((s, hv), F32)],
        grid=(s // tm,),
        in_specs=[row(d), _full(mod.shape), _full(ng.shape), row(LANES), row(LANES), _full(w.shape)],
        out_specs=[row(hk), row(hk), row(hv), row(hv)],
        compiler_params=_params("arbitrary"),
        name="ret_in",
    )(x, mod, ng, cos, sin, w)


def _ret_core_kernel(lg_ref, q_ref, k_ref, v_ref, g_ref, gn_ref, o_ref, state_ref, decay_ref, xi_ref, zeta_ref, *,
                     chunk):
    @pl.when(pl.program_id(0) == 0)
    def _():
        state_ref[...] = jnp.zeros(state_ref.shape, F32)
        row = lax.broadcasted_iota(jnp.int32, (chunk, chunk), 0)
        col = lax.broadcasted_iota(jnp.int32, (chunk, chunk), 1)
        diff = (row - col).astype(F32)
        idx = lax.broadcasted_iota(jnp.int32, (chunk, LANES), 0).astype(F32)
        for hd in range(RET_HEADS):
            lg_row = lg_ref[hd][0:1, :]
            decay_ref[hd] = jnp.where(diff >= 0.0,
                                      jnp.exp(jnp.maximum(diff, 0.0) * _tile_lanes(lg_row, chunk)), 0.0)
            xi_ref[hd] = jnp.exp((idx + 1.0) * lg_row)
            zeta_ref[hd] = jnp.exp((chunk - 1.0 - idx) * lg_row)

    for hd in range(RET_HEADS):
        decay, xi, zeta = decay_ref[hd], xi_ref[hd], zeta_ref[hd]
        g_chunk = jnp.exp(chunk * lg_ref[hd])
        q = q_ref[:, hd * RET_KEY_DIM:(hd + 1) * RET_KEY_DIM]
        k = k_ref[:, hd * RET_KEY_DIM:(hd + 1) * RET_KEY_DIM]
        v = v_ref[:, hd * RET_VAL_DIM:(hd + 1) * RET_VAL_DIM]
        state = state_ref[hd]
        sc = _dot_nt(q, k) * decay
        inner = _dot(sc.astype(BF16), v)
        cross = _dot(q, state.astype(BF16)) * _tile_lanes(xi, RET_VAL_DIM)
        kz = (k.astype(F32) * _tile_lanes(zeta, RET_KEY_DIM)).astype(BF16)
        upd = lax.dot_general(kz, v, (((0,), (0,)), ((), ())), preferred_element_type=F32)
        state_ref[hd] = state * _tile_lanes(g_chunk[0:1, :], RET_VAL_DIM) + upd
        o = inner + cross
        o = o * lax.rsqrt(jnp.mean(o * o, axis=-1, keepdims=True) + EPS)
        o = o * gn_ref[:, hd * RET_VAL_DIM:(hd + 1) * RET_VAL_DIM]
        g = g_ref[:, hd * RET_VAL_DIM:(hd + 1) * RET_VAL_DIM]
        o_ref[:, hd * RET_VAL_DIM:(hd + 1) * RET_VAL_DIM] = (g * (1.0 / (1.0 + jnp.exp(-g))) * o).astype(o_ref.dtype)


def _ret_core(q, k, v, g, gn_g):
    s = q.shape[0]
    chunk = RET_CHUNK
    hk, hv = RET_HEADS * RET_KEY_DIM, RET_HEADS * RET_VAL_DIM
    log_gamma = jnp.log(1.0 - 2.0 ** (-5.0 - jnp.arange(RET_HEADS, dtype=F32)))
    lg = jnp.broadcast_to(log_gamma[:, None, None], (RET_HEADS, SUBLANES, LANES))
    row = lambda w: pl.BlockSpec((chunk, w), lambda i: (i, 0))
    return pl.pallas_call(
        functools.partial(_ret_core_kernel, chunk=chunk),
        out_shape=jax.ShapeDtypeStruct((s, hv), BF16),
        grid=(s // chunk,),
        in_specs=[_full(lg.shape), row(hk), row(hk), row(hv), row(hv), _full((1, hv))],
        out_specs=row(hv),
        scratch_shapes=[pltpu.VMEM((RET_HEADS, RET_KEY_DIM, RET_VAL_DIM), F32),
                        pltpu.VMEM((RET_HEADS, chunk, chunk), F32), pltpu.VMEM((RET_HEADS, chunk, LANES), F32),
                        pltpu.VMEM((RET_HEADS, chunk, LANES), F32)],
        compiler_params=_params("arbitrary"),
        name="ret_core",
    )(lg, q, k, v, g, gn_g.reshape(1, hv))


def _post_kernel(o_ref, x_ref, mod_ref, ng_ref, wo_ref, w1_ref, w2_ref, out_ref, y_ref):
    x = x_ref[...]
    y = _dot(o_ref[...], wo_ref[...])
    x = x + mod_ref[2:3, :] * _rms(y, ng_ref[1:2, :])
    h = _modulated(x, ng_ref[2:3, :], mod_ref[3:4, :], mod_ref[4:5, :])
    d_ff = w1_ref.shape[1]
    for c in range(d_ff // FF_CHUNK):
        a = jnp.maximum(_dot(h, w1_ref[:, c * FF_CHUNK:(c + 1) * FF_CHUNK]), 0.0)
        part = _dot((a * a).astype(BF16), w2_ref[c * FF_CHUNK:(c + 1) * FF_CHUNK, :])
        if c == 0:
            y_ref[...] = part
        else:
            y_ref[...] += part
    out_ref[...] = x + mod_ref[5:6, :] * _rms(y_ref[...], ng_ref[3:4, :])


def _post(o, x, mod, ng, w_out, w1_all, w2_all, layer):
    s, d = x.shape
    tm = ROW_TILE
    ko = o.shape[1]
    row = lambda w: pl.BlockSpec((tm, w), lambda i: (i, 0))
    resident = lambda a: pl.BlockSpec(a.shape, lambda i: (0, 0), pipeline_mode=pl.Buffered(1))
    of_layer = lambda a: pl.BlockSpec((None,) + a.shape[1:], lambda i: (layer, 0, 0), pipeline_mode=pl.Buffered(1))
    wo = w_out.astype(BF16)
    return pl.pallas_call(
        _post_kernel,
        out_shape=jax.ShapeDtypeStruct((s, d), F32),
        grid=(s // tm,),
        in_specs=[row(ko), row(d), _full(mod.shape), _full(ng.shape), resident(wo), of_layer(w1_all),
                  of_layer(w2_all)],
        out_specs=row(d),
        scratch_shapes=[pltpu.VMEM((tm, d), F32)],
        compiler_params=_params("arbitrary"),
        name="post",
    )(o, x, mod, ng, wo, w1_all, w2_all)


def kernel(x, c, positions, ada_w, ada_b, norm_g, ffn_w1, ffn_w2, mla_w_in, mla_q_norm, mla_kv_norm, mla_w_q_up,
           mla_w_kv_up, mla_w_out, sb_w_in, sb_w_out, ret_w_in, ret_gn_g, ret_w_out):
    batch, seq, d = x.shape
    assert batch == 1 and seq % MLA_Q_BLOCK == 0 and seq % ROW_TILE == 0
    depth = ada_w.shape[0]
    mod = _adaln(c, ada_w, ada_b)
    tables = _rope_tables(positions)
    w1_all, w2_all = ffn_w1.astype(BF16), ffn_w2.astype(BF16)
    xs = x.reshape(seq, d)
    for i in range(depth):
        kind, j = i % 3, i // 3
        if kind == 0:
            o = _mla_mixer(xs, mod[i], norm_g[i], tables, mla_w_in[j], mla_q_norm[j], mla_kv_norm[j],
                           mla_w_q_up[j], mla_w_kv_up[j])
            w_out = mla_w_out[j]
        elif kind == 1:
            q, k, v = _sb_in(xs, mod[i], norm_g[i], sb_w_in[j])
            o, w_out = _sb_attn(q, k, v), sb_w_out[j]
        else:
            q, k, v, g = _ret_in(xs, mod[i], norm_g[i], tables[2], tables[3], ret_w_in[j])
            o, w_out = _ret_core(q, k, v, g, ret_gn_g[j]), ret_w_out[j]
        xs = _post(o, xs, mod[i], norm_g[i], w_out, w1_all, w2_all, i)
    return xs.reshape(batch, seq, d)
```

```python
import functools
import math

import jax
import jax.numpy as jnp
from jax import lax
from jax.experimental import pallas as pl
from jax.experimental.pallas import tpu as pltpu

EPS = 1e-6
ROPE_BASE = 10000.0
MASK_VALUE = -1e30
LOG2E = math.log2(math.e)

LANES = 128
SUBLANES = 8
VMEM_LIMIT_BYTES = 60000 * 1024

MLA_HEADS = 8
MLA_Q_RANK = 256
MLA_KV_RANK = 128
MLA_NOPE = 128
MLA_ROPE = 64
MLA_V = 128
MLA_QK_PAD = 256

SB_HEADS = 16
SB_HEAD_DIM = 64

RET_HEADS = 4
RET_KEY_DIM = 256
RET_VAL_DIM = 512

ROW_TILE = 512
FF_CHUNK = 1024
MLA_BLOCK = 512
MLA_Q_BLOCK = 2048
MLA_Q_GROUP = 256
SB_Q_BLOCK = 256
SB_K_BLOCK = 256
SB_HEADS_PER_STEP = 8
SB_GROUP_HEADS = 4
SB_EXP2_CLAMP = 126.0
SB_EXP2_UNDERFLOW = 160.0
RET_CHUNK = 256

F32 = jnp.float32
BF16 = jnp.bfloat16


def _params(*semantics):
    return pltpu.CompilerParams(dimension_semantics=semantics, vmem_limit_bytes=VMEM_LIMIT_BYTES)


def _full(shape):
    nd = len(shape)
    return pl.BlockSpec(shape, lambda *_: (0,) * nd)


def _rms(x, g):
    return x * lax.rsqrt(jnp.mean(x * x, axis=-1, keepdims=True) + EPS) * g


def _dot(a, b):
    return jnp.dot(a, b, preferred_element_type=F32)


def _dot_nt(a, b):
    return lax.dot_general(a, b, (((1,), (1,)), ((), ())), preferred_element_type=F32)


def _tile_lanes(v, width):
    reps = width // LANES
    return v if reps == 1 else jnp.concatenate([v] * reps, axis=1)


def _adaln_kernel(c_ref, w_ref, b_ref, o_ref):
    c = c_ref[...]
    cond = c * (1.0 / (1.0 + jnp.exp(-c)))
    o_ref[0] = jnp.sum(w_ref[0] * cond, axis=0, keepdims=True) + b_ref[0]


def _adaln(c, ada_w, ada_b):
    depth, d, n = ada_w.shape
    tn = 768
    out = pl.pallas_call(
        _adaln_kernel,
        out_shape=jax.ShapeDtypeStruct((depth, 1, n), F32),
        grid=(depth, n // tn),
        in_specs=[
            _full((d, 1)),
            pl.BlockSpec((1, d, tn), lambda i, j: (i, 0, j)),
            pl.BlockSpec((1, 1, tn), lambda i, j: (i, 0, j)),
        ],
        out_specs=pl.BlockSpec((1, 1, tn), lambda i, j: (i, 0, j)),
        compiler_params=_params("arbitrary", "arbitrary"),
        name="adaln",
    )(c.reshape(d, 1), ada_w, ada_b.reshape(depth, 1, n))
    return out.reshape(depth, 6, d)


def _rope_kernel(pos_ref, f64_ref, f256_ref, a_mask_ref, b_sign_ref, a_ref, b_ref, cos_ref, sin_ref):
    pos = pos_ref[...]
    ang = pos * f64_ref[...]
    a_ref[...] = jnp.cos(ang) * a_mask_ref[...]
    b_ref[...] = jnp.sin(ang) * b_sign_ref[...]
    ang = pos * f256_ref[...]
    cos_ref[...] = jnp.cos(ang)
    sin_ref[...] = jnp.sin(ang)


def _rope_tables(positions):
    s = positions.shape[-1]
    tm = 1024
    pos = positions.astype(F32).reshape(s, 1)
    f64 = ROPE_BASE ** (-jnp.arange(0, MLA_ROPE, 2, dtype=F32) / MLA_ROPE)
    f256 = ROPE_BASE ** (-jnp.arange(0, RET_KEY_DIM, 2, dtype=F32) / RET_KEY_DIM)
    quarter = MLA_ROPE // 2
    ones, zeros = jnp.ones((quarter,), F32), jnp.zeros((quarter,), F32)
    a_mask = jnp.concatenate([ones, ones, zeros, zeros]).reshape(1, LANES)
    b_sign = jnp.concatenate([-ones, ones, zeros, zeros]).reshape(1, LANES)
    row = pl.BlockSpec((tm, LANES), lambda i: (i, 0))
    return pl.pallas_call(
        _rope_kernel,
        out_shape=[jax.ShapeDtypeStruct((s, LANES), F32)] * 4,
        grid=(s // tm,),
        in_specs=[pl.BlockSpec((tm, 1), lambda i: (i, 0))] + [_full((1, LANES))] * 4,
        out_specs=[row] * 4,
        compiler_params=_params("arbitrary"),
        name="rope_tables",
    )(pos, jnp.tile(f64, 4).reshape(1, LANES), f256.reshape(1, LANES), a_mask, b_sign)


def _modulated(x, g, shift, scale):
    return (_rms(x, g) * (1.0 + scale) + shift).astype(BF16)


def _mla_in_kernel(x_ref, mod_ref, ng_ref, a_ref, b_ref, w_in_ref, qn_ref, kvn_ref, wq_ref, wk_ref, wv_ref,
                   q_ref, k_ref, v_ref, *, q_scale):
    h = _modulated(x_ref[...], ng_ref[0:1, :], mod_ref[0:1, :], mod_ref[1:2, :])
    lat = _dot(h, w_in_ref[...])
    rope_a, rope_b = a_ref[...], b_ref[...]

    def rope(v):
        return v * rope_a + pltpu.roll(v, 2 * (MLA_ROPE // 2), axis=1) * rope_b

    q_lat = lat[:, :MLA_Q_RANK]
    kv_lat = lat[:, MLA_Q_RANK:MLA_Q_RANK + MLA_KV_RANK]
    k_pe = rope(lat[:, MLA_Q_RANK + MLA_KV_RANK:]).astype(BF16)
    q = _dot(_rms(q_lat, qn_ref[...]).astype(BF16), wq_ref[...])
    kvn = _rms(kv_lat, kvn_ref[...]).astype(BF16)
    k_nope = _dot(kvn, wk_ref[...]).astype(BF16)
    v_t = _dot_nt(wv_ref[...], kvn).astype(BF16)
    v_ref[...] = v_t.reshape(MLA_HEADS, 1, MLA_V, v_t.shape[1])
    q_parts, k_parts = [], []
    for hd in range(MLA_HEADS):
        lo = hd * MLA_QK_PAD
        q_parts.append((q[:, lo:lo + MLA_NOPE] * q_scale).astype(BF16))
        q_parts.append((rope(q[:, lo + MLA_NOPE:lo + MLA_QK_PAD]) * q_scale).astype(BF16))
        k_parts.append(k_nope[:, hd * MLA_NOPE:(hd + 1) * MLA_NOPE])
        k_parts.append(k_pe)
    q_ref[...] = jnp.concatenate(q_parts, axis=1)
    k_ref[...] = jnp.concatenate(k_parts, axis=1)


def _mla_weights(w_in, w_q_up, w_kv_up):
    half = MLA_ROPE // 2
    pe = w_in[:, MLA_Q_RANK + MLA_KV_RANK:]
    pe1, pe2 = pe[:, :half], pe[:, half:]
    w_in_p = jnp.concatenate([w_in[:, :MLA_Q_RANK + MLA_KV_RANK], pe1, pe2, pe2, pe1], axis=1)
    wq = w_q_up.reshape(MLA_Q_RANK, MLA_HEADS, MLA_NOPE + MLA_ROPE)
    q1, q2 = wq[..., MLA_NOPE:MLA_NOPE + half], wq[..., MLA_NOPE + half:]
    wq_p = jnp.concatenate([wq[..., :MLA_NOPE], q1, q2, q2, q1], axis=-1).reshape(MLA_Q_RANK, MLA_HEADS * MLA_QK_PAD)
    wkv = w_kv_up.reshape(MLA_KV_RANK, MLA_HEADS, MLA_NOPE + MLA_V)
    wk = wkv[..., :MLA_NOPE].reshape(MLA_KV_RANK, MLA_HEADS * MLA_NOPE)
    wv_t = wkv[..., MLA_NOPE:].reshape(MLA_KV_RANK, MLA_HEADS * MLA_V).T
    return w_in_p.astype(BF16), wq_p.astype(BF16), wk.astype(BF16), wv_t.astype(BF16)


def _mla_in(x, mod, ng, rope_a, rope_b, w_in, q_norm, kv_norm, w_q_up, w_kv_up):
    s, d = x.shape
    tm = MLA_BLOCK
    w_in_p, wq_p, wk, wv_t = _mla_weights(w_in, w_q_up, w_kv_up)
    q_scale = (MLA_NOPE + MLA_ROPE) ** -0.5 * LOG2E
    row = lambda n: pl.BlockSpec((tm, n), lambda i: (i, 0))
    hq = MLA_HEADS * MLA_QK_PAD
    return pl.pallas_call(
        functools.partial(_mla_in_kernel, q_scale=q_scale),
        out_shape=[jax.ShapeDtypeStruct((s, hq), BF16), jax.ShapeDtypeStruct((s, hq), BF16),
                   jax.ShapeDtypeStruct((MLA_HEADS, s // tm, MLA_V, tm), BF16)],
        grid=(s // tm,),
        in_specs=[row(d), _full(mod.shape), _full(ng.shape), row(LANES), row(LANES), _full(w_in_p.shape),
                  _full((1, MLA_Q_RANK)), _full((1, MLA_KV_RANK)), _full(wq_p.shape), _full(wk.shape),
                  _full(wv_t.shape)],
        out_specs=[row(hq), row(hq), pl.BlockSpec((MLA_HEADS, 1, MLA_V, tm), lambda i: (0, i, 0, 0))],
        compiler_params=_params("arbitrary"),
        name="mla_in",
    )(x, mod, ng, rope_a, rope_b, w_in_p, q_norm.reshape(1, -1), kv_norm.reshape(1, -1), wq_p, wk, wv_t)


def _interleave(producers, consumers, lead=2):
    order = list(producers[:lead])
    rest = list(producers[lead:])
    for f in consumers:
        order.append(f)
        if rest:
            order.append(rest.pop(0))
    return order + rest


def _mla_attn_kernel(q_ref, k_ref, v_ref, o_ref, *scratch, bq, bk, sub):
    i = pl.program_id(1)
    groups = bq // sub
    ratio = bq // bk
    assert bq == ratio * bk and ratio % 2 == 0
    m_refs, l_refs = scratch[:groups], scratch[groups:2 * groups]
    acc_refs = scratch[2 * groups:3 * groups]
    s_refs = (scratch[3 * groups:4 * groups], scratch[4 * groups:5 * groups])
    qt_ref = scratch[5 * groups]
    qt_ref[...] = q_ref[...].astype(F32).T.astype(BF16)
    for c in range(groups):
        m_refs[c][...] = jnp.full((1, sub), MASK_VALUE, F32)
        l_refs[c][...] = jnp.zeros((1, sub), F32)
        acc_refs[c][...] = jnp.zeros((MLA_V, sub), F32)

    def offset_of(c, diag):
        return None if diag is None else c * sub - diag * bk

    def live_groups(diag):
        return [c for c in range(groups) if diag is None or offset_of(c, diag) >= -(sub - 1)]

    def producers(slot, j, diag):
        start = pl.multiple_of(j * bk, bk)
        k = k_ref[pl.ds(start, bk), :]

        def make(c):
            def produce():
                s_refs[slot][c][...] = _dot(k, qt_ref[:, c * sub:(c + 1) * sub])
            return produce
        return [make(c) for c in live_groups(diag)]

    def consumers(slot, j, diag):
        v = v_ref[0, j]

        def make(c):
            offset = offset_of(c, diag)

            def consume():
                s = s_refs[slot][c][...]
                if offset is not None and offset < bk - 1:
                    key = lax.broadcasted_iota(jnp.int32, (bk, sub), 0)
                    qry = lax.broadcasted_iota(jnp.int32, (bk, sub), 1)
                    s = jnp.where(key - qry <= offset, s, MASK_VALUE)
                m_prev = m_refs[c][...]
                m_new = jnp.maximum(m_prev, jnp.max(s, axis=0, keepdims=True))
                alpha = jnp.exp2(m_prev - m_new)
                p = jnp.exp2(s - m_new)
                l_refs[c][...] = alpha * l_refs[c][...] + jnp.sum(p, axis=0, keepdims=True)
                m_refs[c][...] = m_new
                acc_refs[c][...] = alpha * acc_refs[c][...] + _dot(v, p.astype(BF16))
            return consume
        return [make(c) for c in live_groups(diag)]

    def stage(slot, j, diag, next_j, next_diag):
        nxt = [] if next_j is None else producers(1 - slot, next_j, next_diag)
        for f in _interleave(nxt, consumers(slot, j, diag)):
            f()

    for f in producers(0, 0, None):
        f()

    def body(t, carry):
        stage(0, 2 * t, None, 2 * t + 1, None)
        stage(1, 2 * t + 1, None, 2 * t + 2, None)
        return carry

    first_diag = i * ratio
    lax.fori_loop(0, first_diag // 2, body, 0)
    for dj in range(ratio):
        last = dj == ratio - 1
        stage(dj % 2, first_diag + dj, dj, None if last else first_diag + dj + 1, None if last else dj + 1)
    for c in range(groups):
        o_ref[c * sub:(c + 1) * sub, :] = (acc_refs[c][...] / l_refs[c][...]).T.astype(o_ref.dtype)


def _mla_attn(q, k, v_t):
    s = q.shape[0]
    bq, bk, sub = MLA_Q_BLOCK, MLA_BLOCK, MLA_Q_GROUP
    groups = bq // sub
    return pl.pallas_call(
        functools.partial(_mla_attn_kernel, bq=bq, bk=bk, sub=sub),
        out_shape=jax.ShapeDtypeStruct((s, MLA_HEADS * MLA_V), BF16),
        grid=(MLA_HEADS, s // bq),
        in_specs=[pl.BlockSpec((bq, MLA_QK_PAD), lambda h, i: (i, h)),
                  pl.BlockSpec((s, MLA_QK_PAD), lambda h, i: (0, h)),
                  pl.BlockSpec((1, s // bk, MLA_V, bk), lambda h, i: (h, 0, 0, 0))],
        out_specs=pl.BlockSpec((bq, MLA_V), lambda h, i: (i, h)),
        scratch_shapes=([pltpu.VMEM((1, sub), F32)] * (2 * groups) + [pltpu.VMEM((MLA_V, sub), F32)] * groups
                        + [pltpu.VMEM((bk, sub), F32)] * (2 * groups) + [pltpu.VMEM((MLA_QK_PAD, bq), BF16)]),
        compiler_params=_params("arbitrary", "arbitrary"),
        name="mla_attn",
    )(q, k, v_t)


def _mla_mixer(x, mod, ng, tables, w_in, q_norm, kv_norm, w_q_up, w_kv_up):
    q, k, v = _mla_in(x, mod, ng, tables[0], tables[1], w_in, q_norm, kv_norm, w_q_up, w_kv_up)
    return _mla_attn(q, k, v)


def _sb_in_kernel(x_ref, mod_ref, ng_ref, wqv_ref, wkt_ref, q_ref, kt_ref, v_ref, *, q_scale, bk):
    h = _modulated(x_ref[...], ng_ref[0:1, :], mod_ref[0:1, :], mod_ref[1:2, :])
    qv = _dot(h, wqv_ref[...])
    n = SB_HEADS * SB_HEAD_DIM
    q_ref[...] = (qv[:, :n] * q_scale).astype(BF16)
    v_ref[...] = qv[:, n:].astype(BF16)
    k_t = _dot_nt(wkt_ref[...], h).astype(BF16)
    for b in range(k_t.shape[1] // bk):
        kt_ref[b] = k_t[:, b * bk:(b + 1) * bk]


def _sb_in(x, mod, ng, w_in):
    s, d = x.shape
    tm, bk = ROW_TILE, SB_K_BLOCK
    n = SB_HEADS * SB_HEAD_DIM
    row = lambda w: pl.BlockSpec((tm, w), lambda i: (i, 0))
    w = w_in.astype(BF16)
    w_qv = jnp.concatenate([w[:, :n], w[:, 2 * n:]], axis=1)
    wk_t = w[:, n:2 * n].T
    return pl.pallas_call(
        functools.partial(_sb_in_kernel, q_scale=-LOG2E * SB_HEAD_DIM ** -0.5, bk=bk),
        out_shape=[jax.ShapeDtypeStruct((s, n), BF16), jax.ShapeDtypeStruct((s // bk, n, bk), BF16),
                   jax.ShapeDtypeStruct((s, n), BF16)],
        grid=(s // tm,),
        in_specs=[row(d), _full(mod.shape), _full(ng.shape), _full(w_qv.shape), _full(wk_t.shape)],
        out_specs=[row(n), pl.BlockSpec((tm // bk, n, bk), lambda i: (i, 0, 0)), row(n)],
        compiler_params=_params("arbitrary"),
        name="sb_in",
    )(x, mod, ng, w_qv, wk_t)


def _sb_attn_kernel(q_ref, kt_ref, v_ref, tri_ref, o_ref, acc_ref, *scratch, bq, bk, heads):
    i = pl.program_id(1)
    ratio = bq // bk
    assert bq == ratio * bk and heads % SB_GROUP_HEADS == 0
    gw = SB_GROUP_HEADS * SB_HEAD_DIM
    c_refs = scratch[:heads]
    w_refs = (scratch[heads:2 * heads], scratch[2 * heads:3 * heads])
    lane_q = lax.broadcasted_iota(jnp.int32, (bq, gw), 1)
    lane_v = lax.broadcasted_iota(jnp.int32, (bk, gw), 1)

    def group_of(hd):
        return slice((hd // SB_GROUP_HEADS) * gw, (hd // SB_GROUP_HEADS + 1) * gw)

    def head_lanes(lane, hd):
        lo = (hd % SB_GROUP_HEADS) * SB_HEAD_DIM
        return (lane >= lo) & (lane < lo + SB_HEAD_DIM)

    def only_head(x, lane, hd):
        return jnp.where(head_lanes(lane, hd), x, jnp.zeros_like(x))

    q_heads = [only_head(q_ref[:, group_of(hd)], lane_q, hd) for hd in range(heads)]
    acc_ref[...] = jnp.zeros(acc_ref.shape, F32)
    for hd in range(heads):
        c_refs[hd][...] = jnp.zeros((bq, LANES), F32)
    tri = tri_ref[...]

    def producers(slot, j):
        k_t = kt_ref[j]

        def make(hd):
            def produce():
                w_refs[slot][hd][...] = _dot(q_heads[hd], k_t[group_of(hd), :])
            return produce
        return [make(hd) for hd in range(heads)]

    def stage(slot, j, mask_offset, next_j):
        start = pl.multiple_of(j * bk, bk)
        v = v_ref[pl.ds(start, bk), :]
        if mask_offset is not None:
            row = lax.broadcasted_iota(jnp.int32, (bq, bk), 0)
            col = lax.broadcasted_iota(jnp.int32, (bq, bk), 1)
            strict = (col - row) < mask_offset
        weights = [None] * heads

        def make(hd):
            def consume():
                w = w_refs[slot][hd][...]
                neg_log_b = jnp.log2(1.0 + jnp.exp2(jnp.minimum(w, SB_EXP2_CLAMP)))
                log_1m = jnp.minimum(w - neg_log_b, 0.0)
                if mask_offset is not None:
                    log_1m = jnp.where(strict, log_1m, 0.0)
                suffix = _dot(log_1m.astype(BF16), tri)
                c = c_refs[hd][...]
                a = jnp.exp2(suffix - neg_log_b + _tile_lanes(c, bk))
                if mask_offset is not None:
                    a = jnp.where(strict, a, 0.0)
                weights[hd] = a.astype(BF16)
                c_refs[hd][...] = c + jnp.sum(log_1m, axis=1, keepdims=True)
            return consume

        for f in _interleave(producers(1 - slot, next_j), [make(hd) for hd in range(heads)]):
            f()
        for g in range(heads // SB_GROUP_HEADS):
            members = range(g * SB_GROUP_HEADS, (g + 1) * SB_GROUP_HEADS)
            v_heads = [only_head(v[:, group_of(hd)], lane_v, hd) for hd in members]
            acc_ref[:, g * gw:(g + 1) * gw] += _dot(jnp.concatenate([weights[hd] for hd in members], axis=1),
                                                    jnp.concatenate(v_heads, axis=0))

    last = ratio * i + ratio - 1
    for f in producers(0, last):
        f()
    for p in range(ratio):
        dj = ratio - 1 - p
        stage(p % 2, ratio * i + dj, -dj * bk, jnp.maximum(ratio * i + dj - 1, 0))

    def any_weight_left():
        worst = c_refs[0][...]
        for hd in range(1, heads):
            worst = jnp.maximum(worst, c_refs[hd][...])
        return (jnp.max(worst) > -SB_EXP2_UNDERFLOW).astype(jnp.int32)

    def more(carry):
        n, left = carry
        return jnp.logical_and(n < ratio * i, left > 0)

    def body(carry):
        n, _ = carry
        j = ratio * i - 1 - n
        nxt = jnp.maximum(j - 1, 0)
        for slot in range(2):
            @pl.when((n + ratio) % 2 == slot)
            def _():
                stage(slot, j, None, nxt)
        return n + 1, any_weight_left()

    lax.while_loop(more, body, (0, any_weight_left()))
    o_ref[...] = acc_ref[...].astype(o_ref.dtype)


def _sb_attn(q, k_t, v):
    s = q.shape[0]
    bq, bk, heads = SB_Q_BLOCK, SB_K_BLOCK, SB_HEADS_PER_STEP
    width = heads * SB_HEAD_DIM
    tri = jnp.tril(jnp.ones((bk, bk), F32), k=-1).astype(BF16)
    return pl.pallas_call(
        functools.partial(_sb_attn_kernel, bq=bq, bk=bk, heads=heads),
        out_shape=jax.ShapeDtypeStruct((s, SB_HEADS * SB_HEAD_DIM), BF16),
        grid=(SB_HEADS // heads, s // bq),
        in_specs=[pl.BlockSpec((bq, width), lambda p, i: (i, p)),
                  pl.BlockSpec((s // bk, width, bk), lambda p, i: (0, p, 0), pipeline_mode=pl.Buffered(1)),
                  pl.BlockSpec((s, width), lambda p, i: (0, p), pipeline_mode=pl.Buffered(1)), _full((bk, bk))],
        out_specs=pl.BlockSpec((bq, width), lambda p, i: (i, p)),
        scratch_shapes=([pltpu.VMEM((bq, width), F32)] + [pltpu.VMEM((bq, LANES), F32)] * heads
                        + [pltpu.VMEM((bq, bk), F32)] * (2 * heads)),
        compiler_params=_params("arbitrary", "arbitrary"),
        name="sb_attn",
    )(q, k_t, v, tri)


def _ret_in_kernel(x_ref, mod_ref, ng_ref, cos_ref, sin_ref, w_ref, q_ref, k_ref, v_ref, g_ref, *, k_scale):
    h = _modulated(x_ref[...], ng_ref[0:1, :], mod_ref[0:1, :], mod_ref[1:2, :])
    y = _dot(h, w_ref[...])
    hk, hv = RET_HEADS * RET_KEY_DIM, RET_HEADS * RET_VAL_DIM
    cos, sin = cos_ref[...], sin_ref[...]
    half = RET_KEY_DIM // 2

    def rope(t, scale):
        parts = []
        for hd in range(RET_HEADS):
            t1 = t[:, hd * RET_KEY_DIM:hd * RET_KEY_DIM + half]
            t2 = t[:, hd * RET_KEY_DIM + half:(hd + 1) * RET_KEY_DIM]
            parts.append(((t1 * cos - t2 * sin) * scale).astype(BF16))
            parts.append(((t2 * cos + t1 * sin) * scale).astype(BF16))
        return jnp.concatenate(parts, axis=1)

    q_ref[...] = rope(y[:, :hk], 1.0)
    k_ref[...] = rope(y[:, hk:2 * hk], k_scale)
    v_ref[...] = y[:, 2 * hk:2 * hk + hv].astype(BF16)
    g_ref[...] = y[:, 2 * hk + hv:]


def _ret_in(x, mod, ng, cos, sin, w_in):
    s, d = x.shape
    tm = ROW_TILE
    hk, hv = RET_HEADS * RET_KEY_DIM, RET_HEADS * RET_VAL_DIM
    row = lambda w: pl.BlockSpec((tm, w), lambda i: (i, 0))
    w = w_in.astype(BF16)
    return pl.pallas_call(
        functools.partial(_ret_in_kernel, k_scale=RET_KEY_DIM ** -0.5),
        out_shape=[jax.ShapeDtypeStruct((s, hk), BF16), jax.ShapeDtypeStruct((s, hk), BF16),
                   jax.ShapeDtypeStruct((s, hv), BF16), jax.ShapeDtypeStruct((s, hv), F32)],
        grid=(s // tm,),
        in_specs=[row(d), _full(mod.shape), _full(ng.shape), row(LANES), row(LANES), _full(w.shape)],
        out_specs=[row(hk), row(hk), row(hv), row(hv)],
        compiler_params=_params("arbitrary"),
        name="ret_in",
    )(x, mod, ng, cos, sin, w)


def _ret_core_kernel(lg_ref, q_ref, k_ref, v_ref, g_ref, gn_ref, o_ref, state_ref, decay_ref, xi_ref, zeta_ref, *,
                     chunk):
    @pl.when(pl.program_id(0) == 0)
    def _():
        state_ref[...] = jnp.zeros(state_ref.shape, F32)
        row = lax.broadcasted_iota(jnp.int32, (chunk, chunk), 0)
        col = lax.broadcasted_iota(jnp.int32, (chunk, chunk), 1)
        diff = (row - col).astype(F32)
        idx = lax.broadcasted_iota(jnp.int32, (chunk, LANES), 0).astype(F32)
        for hd in range(RET_HEADS):
            lg_row = lg_ref[hd][0:1, :]
            decay_ref[hd] = jnp.where(diff >= 0.0,
                                      jnp.exp(jnp.maximum(diff, 0.0) * _tile_lanes(lg_row, chunk)), 0.0)
            xi_ref[hd] = jnp.exp((idx + 1.0) * lg_row)
            zeta_ref[hd] = jnp.exp((chunk - 1.0 - idx) * lg_row)

    for hd in range(RET_HEADS):
        decay, xi, zeta = decay_ref[hd], xi_ref[hd], zeta_ref[hd]
        g_chunk = jnp.exp(chunk * lg_ref[hd])
        q = q_ref[:, hd * RET_KEY_DIM:(hd + 1) * RET_KEY_DIM]
        k = k_ref[:, hd * RET_KEY_DIM:(hd + 1) * RET_KEY_DIM]
        v = v_ref[:, hd * RET_VAL_DIM:(hd + 1) * RET_VAL_DIM]
        state = state_ref[hd]
        sc = _dot_nt(q, k) * decay
        inner = _dot(sc.astype(BF16), v)
        cross = _dot(q, state.astype(BF16)) * _tile_lanes(xi, RET_VAL_DIM)
        kz = (k.astype(F32) * _tile_lanes(zeta, RET_KEY_DIM)).astype(BF16)
        upd = lax.dot_general(kz, v, (((0,), (0,)), ((), ())), preferred_element_type=F32)
        state_ref[hd] = state * _tile_lanes(g_chunk[0:1, :], RET_VAL_DIM) + upd
        o = inner + cross
        o = o * lax.rsqrt(jnp.mean(o * o, axis=-1, keepdims=True) + EPS)
        o = o * gn_ref[:, hd * RET_VAL_DIM:(hd + 1) * RET_VAL_DIM]
        g = g_ref[:, hd * RET_VAL_DIM:(hd + 1) * RET_VAL_DIM]
        o_ref[:, hd * RET_VAL_DIM:(hd + 1) * RET_VAL_DIM] = (g * (1.0 / (1.0 + jnp.exp(-g))) * o).astype(o_ref.dtype)


def _ret_core(q, k, v, g, gn_g):
    s = q.shape[0]
    chunk = RET_CHUNK
    hk, hv = RET_HEADS * RET_KEY_DIM, RET_HEADS * RET_VAL_DIM
    log_gamma = jnp.log(1.0 - 2.0 ** (-5.0 - jnp.arange(RET_HEADS, dtype=F32)))
    lg = jnp.broadcast_to(log_gamma[:, None, None], (RET_HEADS, SUBLANES, LANES))
    row = lambda w: pl.BlockSpec((chunk, w), lambda i: (i, 0))
    return pl.pallas_call(
        functools.partial(_ret_core_kernel, chunk=chunk),
        out_shape=jax.ShapeDtypeStruct((s, hv), BF16),
        grid=(s // chunk,),
        in_specs=[_full(lg.shape), row(hk), row(hk), row(hv), row(hv), _full((1, hv))],
        out_specs=row(hv),
        scratch_shapes=[pltpu.VMEM((RET_HEADS, RET_KEY_DIM, RET_VAL_DIM), F32),
                        pltpu.VMEM((RET_HEADS, chunk, chunk), F32), pltpu.VMEM((RET_HEADS, chunk, LANES), F32),
                        pltpu.VMEM((RET_HEADS, chunk, LANES), F32)],
        compiler_params=_params("arbitrary"),
        name="ret_core",
    )(lg, q, k, v, g, gn_g.reshape(1, hv))


def _post_kernel(o_ref, x_ref, mod_ref, ng_ref, wo_ref, w1_ref, w2_ref, out_ref, y_ref):
    x = x_ref[...]
    y = _dot(o_ref[...], wo_ref[...])
    x = x + mod_ref[2:3, :] * _rms(y, ng_ref[1:2, :])
    h = _modulated(x, ng_ref[2:3, :], mod_ref[3:4, :], mod_ref[4:5, :])
    d_ff = w1_ref.shape[1]
    for c in range(d_ff // FF_CHUNK):
        a = jnp.maximum(_dot(h, w1_ref[:, c * FF_CHUNK:(c + 1) * FF_CHUNK]), 0.0)
        part = _dot((a * a).astype(BF16), w2_ref[c * FF_CHUNK:(c + 1) * FF_CHUNK, :])
        if c == 0:
            y_ref[...] = part
        else:
            y_ref[...] += part
    out_ref[...] = x + mod_ref[5:6, :] * _rms(y_ref[...], ng_ref[3:4, :])


def _post(o, x, mod, ng, w_out, w1_all, w2_all, layer):
    s, d = x.shape
    tm = ROW_TILE
    ko = o.shape[1]
    row = lambda w: pl.BlockSpec((tm, w), lambda i: (i, 0))
    resident = lambda a: pl.BlockSpec(a.shape, lambda i: (0, 0), pipeline_mode=pl.Buffered(1))
    of_layer = lambda a: pl.BlockSpec((None,) + a.shape[1:], lambda i: (layer, 0, 0), pipeline_mode=pl.Buffered(1))
    wo = w_out.astype(BF16)
    return pl.pallas_call(
        _post_kernel,
        out_shape=jax.ShapeDtypeStruct((s, d), F32),
        grid=(s // tm,),
        in_specs=[row(ko), row(d), _full(mod.shape), _full(ng.shape), resident(wo), of_layer(w1_all),
                  of_layer(w2_all)],
        out_specs=row(d),
        scratch_shapes=[pltpu.VMEM((tm, d), F32)],
        compiler_params=_params("arbitrary"),
        name="post",
    )(o, x, mod, ng, wo, w1_all, w2_all)


def kernel(x, c, positions, ada_w, ada_b, norm_g, ffn_w1, ffn_w2, mla_w_in, mla_q_norm, mla_kv_norm, mla_w_q_up,
           mla_w_kv_up, mla_w_out, sb_w_in, sb_w_out, ret_w_in, ret_gn_g, ret_w_out):
    batch, seq, d = x.shape
    assert batch == 1 and seq % MLA_Q_BLOCK == 0 and seq % ROW_TILE == 0
    depth = ada_w.shape[0]
    mod = _adaln(c, ada_w, ada_b)
    tables = _rope_tables(positions)
    w1_all, w2_all = ffn_w1.astype(BF16), ffn_w2.astype(BF16)
    xs = x.reshape(seq, d)
    for i in range(depth):
        kind, j = i % 3, i // 3
        if kind == 0:
            o = _mla_mixer(xs, mod[i], norm_g[i], tables, mla_w_in[j], mla_q_norm[j], mla_kv_norm[j],
                           mla_w_q_up[j], mla_w_kv_up[j])
            w_out = mla_w_out[j]
        elif kind == 1:
            q, k, v = _sb_in(xs, mod[i], norm_g[i], sb_w_in[j])
            o, w_out = _sb_attn(q, k, v), sb_w_out[j]
        else:
            q, k, v, g = _ret_in(xs, mod[i], norm_g[i], tables[2], tables[3], ret_w_in[j])
            o, w_out = _ret_core(q, k, v, g, ret_gn_g[j]), ret_w_out[j]
        xs = _post(o, xs, mod[i], norm_g[i], w_out, w1_all, w2_all, i)
    return xs.reshape(batch, seq, d)
```

```python
import functools
import math

import jax
import jax.numpy as jnp
from jax import lax
from jax.experimental import pallas as pl
from jax.experimental.pallas import tpu as pltpu

EPS = 1e-6
ROPE_BASE = 10000.0
MASK_VALUE = -1e30
LOG2E = math.log2(math.e)

LANES = 128
SUBLANES = 8
VMEM_LIMIT_BYTES = 60000 * 1024

MLA_HEADS = 8
MLA_Q_RANK = 256
MLA_KV_RANK = 128
MLA_NOPE = 128
MLA_ROPE = 64
MLA_V = 128
MLA_QK_PAD = 256

SB_HEADS = 16
SB_HEAD_DIM = 64

RET_HEADS = 4
RET_KEY_DIM = 256
RET_VAL_DIM = 512

ROW_TILE = 512
POST_ROW_TILE = 1024
FF_CHUNK = 1024
MLA_BLOCK = 512
MLA_Q_BLOCK = 2048
MLA_Q_GROUP = 256
SB_Q_BLOCK = 256
SB_K_BLOCK = 256
SB_HEADS_PER_STEP = 8
SB_GROUP_HEADS = 4
SB_EXP2_CLAMP = 126.0
SB_EXP2_UNDERFLOW = 160.0
RET_CHUNK = 256

F32 = jnp.float32
BF16 = jnp.bfloat16


def _params(*semantics):
    return pltpu.CompilerParams(dimension_semantics=semantics, vmem_limit_bytes=VMEM_LIMIT_BYTES)


def _full(shape):
    nd = len(shape)
    return pl.BlockSpec(shape, lambda *_: (0,) * nd)


def _rms(x, g):
    return x * lax.rsqrt(jnp.mean(x * x, axis=-1, keepdims=True) + EPS) * g


def _dot(a, b):
    return jnp.dot(a, b, preferred_element_type=F32)


def _dot_nt(a, b):
    return lax.dot_general(a, b, (((1,), (1,)), ((), ())), preferred_element_type=F32)


def _tile_lanes(v, width):
    reps = width // LANES
    return v if reps == 1 else jnp.concatenate([v] * reps, axis=1)


def _adaln_kernel(c_ref, w_ref, b_ref, o_ref):
    c = c_ref[...]
    cond = c * (1.0 / (1.0 + jnp.exp(-c)))
    o_ref[0] = jnp.sum(w_ref[0] * cond, axis=0, keepdims=True) + b_ref[0]


def _adaln(c, ada_w, ada_b):
    depth, d, n = ada_w.shape
    tn = 768
    out = pl.pallas_call(
        _adaln_kernel,
        out_shape=jax.ShapeDtypeStruct((depth, 1, n), F32),
        grid=(depth, n // tn),
        in_specs=[
            _full((d, 1)),
            pl.BlockSpec((1, d, tn), lambda i, j: (i, 0, j)),
            pl.BlockSpec((1, 1, tn), lambda i, j: (i, 0, j)),
        ],
        out_specs=pl.BlockSpec((1, 1, tn), lambda i, j: (i, 0, j)),
        compiler_params=_params("arbitrary", "arbitrary"),
        name="adaln",
    )(c.reshape(d, 1), ada_w, ada_b.reshape(depth, 1, n))
    return out.reshape(depth, 6, d)


def _rope_kernel(pos_ref, f64_ref, f256_ref, a_mask_ref, b_sign_ref, a_ref, b_ref, cos_ref, sin_ref):
    pos = pos_ref[...]
    ang = pos * f64_ref[...]
    a_ref[...] = jnp.cos(ang) * a_mask_ref[...]
    b_ref[...] = jnp.sin(ang) * b_sign_ref[...]
    ang = pos * f256_ref[...]
    cos_ref[...] = jnp.cos(ang)
    sin_ref[...] = jnp.sin(ang)


def _rope_tables(positions):
    s = positions.shape[-1]
    tm = 1024
    pos = positions.astype(F32).reshape(s, 1)
    f64 = ROPE_BASE ** (-jnp.arange(0, MLA_ROPE, 2, dtype=F32) / MLA_ROPE)
    f256 = ROPE_BASE ** (-jnp.arange(0, RET_KEY_DIM, 2, dtype=F32) / RET_KEY_DIM)
    quarter = MLA_ROPE // 2
    ones, zeros = jnp.ones((quarter,), F32), jnp.zeros((quarter,), F32)
    a_mask = jnp.concatenate([ones, ones, zeros, zeros]).reshape(1, LANES)
    b_sign = jnp.concatenate([-ones, ones, zeros, zeros]).reshape(1, LANES)
    row = pl.BlockSpec((tm, LANES), lambda i: (i, 0))
    return pl.pallas_call(
        _rope_kernel,
        out_shape=[jax.ShapeDtypeStruct((s, LANES), F32)] * 4,
        grid=(s // tm,),
        in_specs=[pl.BlockSpec((tm, 1), lambda i: (i, 0))] + [_full((1, LANES))] * 4,
        out_specs=[row] * 4,
        compiler_params=_params("arbitrary"),
        name="rope_tables",
    )(pos, jnp.tile(f64, 4).reshape(1, LANES), f256.reshape(1, LANES), a_mask, b_sign)


def _modulated(x, g, shift, scale):
    return (_rms(x, g) * (1.0 + scale) + shift).astype(BF16)


def _mla_in_kernel(x_ref, mod_ref, ng_ref, a_ref, b_ref, w_in_ref, qn_ref, kvn_ref, wq_ref, wk_ref, wv_ref,
                   q_ref, k_ref, v_ref, *, q_scale):
    h = _modulated(x_ref[...], ng_ref[0:1, :], mod_ref[0:1, :], mod_ref[1:2, :])
    lat = _dot(h, w_in_ref[...])
    rope_a, rope_b = a_ref[...], b_ref[...]

    def rope(v):
        return v * rope_a + pltpu.roll(v, 2 * (MLA_ROPE // 2), axis=1) * rope_b

    q_lat = lat[:, :MLA_Q_RANK]
    kv_lat = lat[:, MLA_Q_RANK:MLA_Q_RANK + MLA_KV_RANK]
    k_pe = rope(lat[:, MLA_Q_RANK + MLA_KV_RANK:]).astype(BF16)
    q = _dot(_rms(q_lat, qn_ref[...]).astype(BF16), wq_ref[...])
    kvn = _rms(kv_lat, kvn_ref[...]).astype(BF16)
    k_nope = _dot(kvn, wk_ref[...]).astype(BF16)
    v_t = _dot_nt(wv_ref[...], kvn).astype(BF16)
    v_ref[...] = v_t.reshape(MLA_HEADS, 1, MLA_V, v_t.shape[1])
    q_parts, k_parts = [], []
    for hd in range(MLA_HEADS):
        lo = hd * MLA_QK_PAD
        q_parts.append((q[:, lo:lo + MLA_NOPE] * q_scale).astype(BF16))
        q_parts.append((rope(q[:, lo + MLA_NOPE:lo + MLA_QK_PAD]) * q_scale).astype(BF16))
        k_parts.append(k_nope[:, hd * MLA_NOPE:(hd + 1) * MLA_NOPE])
        k_parts.append(k_pe)
    q_ref[...] = jnp.concatenate(q_parts, axis=1)
    k_ref[...] = jnp.concatenate(k_parts, axis=1)


def _mla_weights(w_in, w_q_up, w_kv_up):
    half = MLA_ROPE // 2
    pe = w_in[:, MLA_Q_RANK + MLA_KV_RANK:]
    pe1, pe2 = pe[:, :half], pe[:, half:]
    w_in_p = jnp.concatenate([w_in[:, :MLA_Q_RANK + MLA_KV_RANK], pe1, pe2, pe2, pe1], axis=1)
    wq = w_q_up.reshape(MLA_Q_RANK, MLA_HEADS, MLA_NOPE + MLA_ROPE)
    q1, q2 = wq[..., MLA_NOPE:MLA_NOPE + half], wq[..., MLA_NOPE + half:]
    wq_p = jnp.concatenate([wq[..., :MLA_NOPE], q1, q2, q2, q1], axis=-1).reshape(MLA_Q_RANK, MLA_HEADS * MLA_QK_PAD)
    wkv = w_kv_up.reshape(MLA_KV_RANK, MLA_HEADS, MLA_NOPE + MLA_V)
    wk = wkv[..., :MLA_NOPE].reshape(MLA_KV_RANK, MLA_HEADS * MLA_NOPE)
    wv_t = wkv[..., MLA_NOPE:].reshape(MLA_KV_RANK, MLA_HEADS * MLA_V).T
    return w_in_p.astype(BF16), wq_p.astype(BF16), wk.astype(BF16), wv_t.astype(BF16)


def _mla_in(x, mod, ng, rope_a, rope_b, w_in, q_norm, kv_norm, w_q_up, w_kv_up):
    s, d = x.shape
    tm = MLA_BLOCK
    w_in_p, wq_p, wk, wv_t = _mla_weights(w_in, w_q_up, w_kv_up)
    q_scale = (MLA_NOPE + MLA_ROPE) ** -0.5 * LOG2E
    row = lambda n: pl.BlockSpec((tm, n), lambda i: (i, 0))
    hq = MLA_HEADS * MLA_QK_PAD
    return pl.pallas_call(
        functools.partial(_mla_in_kernel, q_scale=q_scale),
        out_shape=[jax.ShapeDtypeStruct((s, hq), BF16), jax.ShapeDtypeStruct((s, hq), BF16),
                   jax.ShapeDtypeStruct((MLA_HEADS, s // tm, MLA_V, tm), BF16)],
        grid=(s // tm,),
        in_specs=[row(d), _full(mod.shape), _full(ng.shape), row(LANES), row(LANES), _full(w_in_p.shape),
                  _full((1, MLA_Q_RANK)), _full((1, MLA_KV_RANK)), _full(wq_p.shape), _full(wk.shape),
                  _full(wv_t.shape)],
        out_specs=[row(hq), row(hq), pl.BlockSpec((MLA_HEADS, 1, MLA_V, tm), lambda i: (0, i, 0, 0))],
        compiler_params=_params("arbitrary"),
        name="mla_in",
    )(x, mod, ng, rope_a, rope_b, w_in_p, q_norm.reshape(1, -1), kv_norm.reshape(1, -1), wq_p, wk, wv_t)


def _interleave(producers, consumers, lead=2):
    order = list(producers[:lead])
    rest = list(producers[lead:])
    for f in consumers:
        order.append(f)
        if rest:
            order.append(rest.pop(0))
    return order + rest


def _mla_attn_kernel(q_ref, k_ref, v_ref, o_ref, *scratch, bq, bk, sub):
    i = pl.program_id(1)
    groups = bq // sub
    ratio = bq // bk
    assert bq == ratio * bk and ratio % 2 == 0
    m_refs, l_refs = scratch[:groups], scratch[groups:2 * groups]
    acc_refs = scratch[2 * groups:3 * groups]
    s_refs = (scratch[3 * groups:4 * groups], scratch[4 * groups:5 * groups])
    qt_ref = scratch[5 * groups]
    qt_ref[...] = q_ref[...].astype(F32).T.astype(BF16)
    for c in range(groups):
        m_refs[c][...] = jnp.full((1, sub), MASK_VALUE, F32)
        l_refs[c][...] = jnp.zeros((1, sub), F32)
        acc_refs[c][...] = jnp.zeros((MLA_V, sub), F32)

    def offset_of(c, diag):
        return None if diag is None else c * sub - diag * bk

    def live_groups(diag):
        return [c for c in range(groups) if diag is None or offset_of(c, diag) >= -(sub - 1)]

    def producers(slot, j, diag):
        start = pl.multiple_of(j * bk, bk)
        k = k_ref[pl.ds(start, bk), :]

        def make(c):
            def produce():
                s_refs[slot][c][...] = _dot(k, qt_ref[:, c * sub:(c + 1) * sub])
            return produce
        return [make(c) for c in live_groups(diag)]

    def consumers(slot, j, diag):
        v = v_ref[0, j]

        def make(c):
            offset = offset_of(c, diag)

            def consume():
                s = s_refs[slot][c][...]
                if offset is not None and offset < bk - 1:
                    key = lax.broadcasted_iota(jnp.int32, (bk, sub), 0)
                    qry = lax.broadcasted_iota(jnp.int32, (bk, sub), 1)
                    s = jnp.where(key - qry <= offset, s, MASK_VALUE)
                m_prev = m_refs[c][...]
                m_new = jnp.maximum(m_prev, jnp.max(s, axis=0, keepdims=True))
                alpha = jnp.exp2(m_prev - m_new)
                p = jnp.exp2(s - m_new)
                l_refs[c][...] = alpha * l_refs[c][...] + jnp.sum(p, axis=0, keepdims=True)
                m_refs[c][...] = m_new
                acc_refs[c][...] = alpha * acc_refs[c][...] + _dot(v, p.astype(BF16))
            return consume
        return [make(c) for c in live_groups(diag)]

    def stage(slot, j, diag, next_j, next_diag):
        nxt = [] if next_j is None else producers(1 - slot, next_j, next_diag)
        for f in _interleave(nxt, consumers(slot, j, diag)):
            f()

    for f in producers(0, 0, None):
        f()

    def body(t, carry):
        for u in range(ratio):
            j = ratio * t + u
            stage(u % 2, j, None, j + 1, None)
        return carry

    first_diag = i * ratio
    lax.fori_loop(0, i, body, 0)
    for dj in range(ratio):
        last = dj == ratio - 1
        stage(dj % 2, first_diag + dj, dj, None if last else first_diag + dj + 1, None if last else dj + 1)
    for c in range(groups):
        o_ref[c * sub:(c + 1) * sub, :] = (acc_refs[c][...] / l_refs[c][...]).T.astype(o_ref.dtype)


def _mla_attn(q, k, v_t):
    s = q.shape[0]
    bq, bk, sub = MLA_Q_BLOCK, MLA_BLOCK, MLA_Q_GROUP
    groups = bq // sub
    return pl.pallas_call(
        functools.partial(_mla_attn_kernel, bq=bq, bk=bk, sub=sub),
        out_shape=jax.ShapeDtypeStruct((s, MLA_HEADS * MLA_V), BF16),
        grid=(MLA_HEADS, s // bq),
        in_specs=[pl.BlockSpec((bq, MLA_QK_PAD), lambda h, i: (i, h)),
                  pl.BlockSpec((s, MLA_QK_PAD), lambda h, i: (0, h)),
                  pl.BlockSpec((1, s // bk, MLA_V, bk), lambda h, i: (h, 0, 0, 0))],
        out_specs=pl.BlockSpec((bq, MLA_V), lambda h, i: (i, h)),
        scratch_shapes=([pltpu.VMEM((1, sub), F32)] * (2 * groups) + [pltpu.VMEM((MLA_V, sub), F32)] * groups
                        + [pltpu.VMEM((bk, sub), F32)] * (2 * groups) + [pltpu.VMEM((MLA_QK_PAD, bq), BF16)]),
        compiler_params=_params("arbitrary", "arbitrary"),
        name="mla_attn",
    )(q, k, v_t)


def _mla_mixer(x, mod, ng, tables, w_in, q_norm, kv_norm, w_q_up, w_kv_up):
    q, k, v = _mla_in(x, mod, ng, tables[0], tables[1], w_in, q_norm, kv_norm, w_q_up, w_kv_up)
    return _mla_attn(q, k, v)


def _sb_in_kernel(x_ref, mod_ref, ng_ref, wqv_ref, wkt_ref, q_ref, kt_ref, v_ref, *, q_scale, bk):
    h = _modulated(x_ref[...], ng_ref[0:1, :], mod_ref[0:1, :], mod_ref[1:2, :])
    qv = _dot(h, wqv_ref[...])
    n = SB_HEADS * SB_HEAD_DIM
    q_ref[...] = (qv[:, :n] * q_scale).astype(BF16)
    v_ref[...] = qv[:, n:].astype(BF16)
    k_t = _dot_nt(wkt_ref[...], h).astype(BF16)
    for b in range(k_t.shape[1] // bk):
        kt_ref[b] = k_t[:, b * bk:(b + 1) * bk]


def _sb_in(x, mod, ng, w_in):
    s, d = x.shape
    tm, bk = ROW_TILE, SB_K_BLOCK
    n = SB_HEADS * SB_HEAD_DIM
    row = lambda w: pl.BlockSpec((tm, w), lambda i: (i, 0))
    w = w_in.astype(BF16)
    w_qv = jnp.concatenate([w[:, :n], w[:, 2 * n:]], axis=1)
    wk_t = w[:, n:2 * n].T
    return pl.pallas_call(
        functools.partial(_sb_in_kernel, q_scale=-LOG2E * SB_HEAD_DIM ** -0.5, bk=bk),
        out_shape=[jax.ShapeDtypeStruct((s, n), BF16), jax.ShapeDtypeStruct((s // bk, n, bk), BF16),
                   jax.ShapeDtypeStruct((s, n), BF16)],
        grid=(s // tm,),
        in_specs=[row(d), _full(mod.shape), _full(ng.shape), _full(w_qv.shape), _full(wk_t.shape)],
        out_specs=[row(n), pl.BlockSpec((tm // bk, n, bk), lambda i: (i, 0, 0)), row(n)],
        compiler_params=_params("arbitrary"),
        name="sb_in",
    )(x, mod, ng, w_qv, wk_t)


def _sb_attn_kernel(q_ref, kt_ref, v_ref, tri_ref, o_ref, acc_ref, *scratch, bq, bk, heads):
    i = pl.program_id(1)
    ratio = bq // bk
    assert bq == ratio * bk and heads % SB_GROUP_HEADS == 0
    gw = SB_GROUP_HEADS * SB_HEAD_DIM
    c_refs = scratch[:heads]
    w_refs = (scratch[heads:2 * heads], scratch[2 * heads:3 * heads])
    lane_q = lax.broadcasted_iota(jnp.int32, (bq, gw), 1)
    lane_v = lax.broadcasted_iota(jnp.int32, (bk, gw), 1)

    def group_of(hd):
        return slice((hd // SB_GROUP_HEADS) * gw, (hd // SB_GROUP_HEADS + 1) * gw)

    def head_lanes(lane, hd):
        lo = (hd % SB_GROUP_HEADS) * SB_HEAD_DIM
        return (lane >= lo) & (lane < lo + SB_HEAD_DIM)

    def only_head(x, lane, hd):
        return jnp.where(head_lanes(lane, hd), x, jnp.zeros_like(x))

    q_heads = [only_head(q_ref[:, group_of(hd)], lane_q, hd) for hd in range(heads)]
    acc_ref[...] = jnp.zeros(acc_ref.shape, F32)
    for hd in range(heads):
        c_refs[hd][...] = jnp.zeros((bq, LANES), F32)
    tri = tri_ref[...]

    def producers(slot, j):
        k_t = kt_ref[j]

        def make(hd):
            def produce():
                w_refs[slot][hd][...] = _dot(q_heads[hd], k_t[group_of(hd), :])
            return produce
        return [make(hd) for hd in range(heads)]

    def stage(slot, j, mask_offset, next_j):
        start = pl.multiple_of(j * bk, bk)
        v = v_ref[pl.ds(start, bk), :]
        if mask_offset is not None:
            row = lax.broadcasted_iota(jnp.int32, (bq, bk), 0)
            col = lax.broadcasted_iota(jnp.int32, (bq, bk), 1)
            strict = (col - row) < mask_offset
        weights = [None] * heads

        def make(hd):
            def consume():
                w = w_refs[slot][hd][...]
                neg_log_b = jnp.log2(1.0 + jnp.exp2(jnp.minimum(w, SB_EXP2_CLAMP)))
                log_1m = jnp.minimum(w - neg_log_b, 0.0)
                if mask_offset is not None:
                    log_1m = jnp.where(strict, log_1m, 0.0)
                suffix = _dot(log_1m.astype(BF16), tri)
                c = c_refs[hd][...]
                a = jnp.exp2(suffix - neg_log_b + _tile_lanes(c, bk))
                if mask_offset is not None:
                    a = jnp.where(strict, a, 0.0)
                weights[hd] = a.astype(BF16)
                c_refs[hd][...] = c + jnp.sum(log_1m, axis=1, keepdims=True)
            return consume

        for f in _interleave(producers(1 - slot, next_j), [make(hd) for hd in range(heads)]):
            f()
        for g in range(heads // SB_GROUP_HEADS):
            members = range(g * SB_GROUP_HEADS, (g + 1) * SB_GROUP_HEADS)
            v_heads = [only_head(v[:, group_of(hd)], lane_v, hd) for hd in members]
            acc_ref[:, g * gw:(g + 1) * gw] += _dot(jnp.concatenate([weights[hd] for hd in members], axis=1),
                                                    jnp.concatenate(v_heads, axis=0))

    last = ratio * i + ratio - 1
    for f in producers(0, last):
        f()
    for p in range(ratio):
        dj = ratio - 1 - p
        stage(p % 2, ratio * i + dj, -dj * bk, jnp.maximum(ratio * i + dj - 1, 0))

    def any_weight_left():
        worst = c_refs[0][...]
        for hd in range(1, heads):
            worst = jnp.maximum(worst, c_refs[hd][...])
        return (jnp.max(worst) > -SB_EXP2_UNDERFLOW).astype(jnp.int32)

    def more(carry):
        n, left = carry
        return jnp.logical_and(n < ratio * i, left > 0)

    def body(carry):
        n, _ = carry
        j = ratio * i - 1 - n
        nxt = jnp.maximum(j - 1, 0)
        for slot in range(2):
            @pl.when((n + ratio) % 2 == slot)
            def _():
                stage(slot, j, None, nxt)
        return n + 1, any_weight_left()

    lax.while_loop(more, body, (0, any_weight_left()))
    o_ref[...] = acc_ref[...].astype(o_ref.dtype)


def _sb_attn(q, k_t, v):
    s = q.shape[0]
    bq, bk, heads = SB_Q_BLOCK, SB_K_BLOCK, SB_HEADS_PER_STEP
    width = heads * SB_HEAD_DIM
    tri = jnp.tril(jnp.ones((bk, bk), F32), k=-1).astype(BF16)
    return pl.pallas_call(
        functools.partial(_sb_attn_kernel, bq=bq, bk=bk, heads=heads),
        out_shape=jax.ShapeDtypeStruct((s, SB_HEADS * SB_HEAD_DIM), BF16),
        grid=(SB_HEADS // heads, s // bq),
        in_specs=[pl.BlockSpec((bq, width), lambda p, i: (i, p)),
                  pl.BlockSpec((s // bk, width, bk), lambda p, i: (0, p, 0), pipeline_mode=pl.Buffered(1)),
                  pl.BlockSpec((s, width), lambda p, i: (0, p), pipeline_mode=pl.Buffered(1)), _full((bk, bk))],
        out_specs=pl.BlockSpec((bq, width), lambda p, i: (i, p)),
        scratch_shapes=([pltpu.VMEM((bq, width), F32)] + [pltpu.VMEM((bq, LANES), F32)] * heads
                        + [pltpu.VMEM((bq, bk), F32)] * (2 * heads)),
        compiler_params=_params("arbitrary", "arbitrary"),
        name="sb_attn",
    )(q, k_t, v, tri)


def _ret_in_kernel(x_ref, mod_ref, ng_ref, cos_ref, sin_ref, w_ref, q_ref, k_ref, v_ref, g_ref, *, k_scale):
    h = _modulated(x_ref[...], ng_ref[0:1, :], mod_ref[0:1, :], mod_ref[1:2, :])
    y = _dot(h, w_ref[...])
    hk, hv = RET_HEADS * RET_KEY_DIM, RET_HEADS * RET_VAL_DIM
    cos, sin = cos_ref[...], sin_ref[...]
    half = RET_KEY_DIM // 2

    def rope(t, scale):
        parts = []
        for hd in range(RET_HEADS):
            t1 = t[:, hd * RET_KEY_DIM:hd * RET_KEY_DIM + half]
            t2 = t[:, hd * RET_KEY_DIM + half:(hd + 1) * RET_KEY_DIM]
            parts.append(((t1 * cos - t2 * sin) * scale).astype(BF16))
            parts.append(((t2 * cos + t1 * sin) * scale).astype(BF16))
        return jnp.concatenate(parts, axis=1)

    q_ref[...] = rope(y[:, :hk], 1.0)
    k_ref[...] = rope(y[:, hk:2 * hk], k_scale)
    v_ref[...] = y[:, 2 * hk:2 * hk + hv].astype(BF16)
    g_ref[...] = y[:, 2 * hk + hv:]


def _ret_in(x, mod, ng, cos, sin, w_in):
    s, d = x.shape
    tm = ROW_TILE
    hk, hv = RET_HEADS * RET_KEY_DIM, RET_HEADS * RET_VAL_DIM
    row = lambda w: pl.BlockSpec((tm, w), lambda i: (i, 0))
    w = w_in.astype(BF16)
    return pl.pallas_call(
        functools.partial(_ret_in_kernel, k_scale=RET_KEY_DIM ** -0.5),
        out_shape=[jax.ShapeDtypeStruct((s, hk), BF16), jax.ShapeDtypeStruct((s, hk), BF16),
                   jax.ShapeDtypeStruct((s, hv), BF16), jax.ShapeDtypeStruct((s, hv), F32)],
        grid=(s // tm,),
        in_specs=[row(d), _full(mod.shape), _full(ng.shape), row(LANES), row(LANES), _full(w.shape)],
        out_specs=[row(hk), row(hk), row(hv), row(hv)],
        compiler_params=_params("arbitrary"),
        name="ret_in",
    )(x, mod, ng, cos, sin, w)


def _ret_core_kernel(lg_ref, q_ref, k_ref, v_ref, g_ref, gn_ref, o_ref, state_ref, decay_ref, xi_ref, zeta_ref, *,
                     chunk):
    @pl.when(pl.program_id(0) == 0)
    def _():
        state_ref[...] = jnp.zeros(state_ref.shape, F32)
        row = lax.broadcasted_iota(jnp.int32, (chunk, chunk), 0)
        col = lax.broadcasted_iota(jnp.int32, (chunk, chunk), 1)
        diff = (row - col).astype(F32)
        idx = lax.broadcasted_iota(jnp.int32, (chunk, LANES), 0).astype(F32)
        for hd in range(RET_HEADS):
            lg_row = lg_ref[hd][0:1, :]
            decay_ref[hd] = jnp.where(diff >= 0.0,
                                      jnp.exp(jnp.maximum(diff, 0.0) * _tile_lanes(lg_row, chunk)), 0.0)
            xi_ref[hd] = jnp.exp((idx + 1.0) * lg_row)
            zeta_ref[hd] = jnp.exp((chunk - 1.0 - idx) * lg_row)

    for hd in range(RET_HEADS):
        decay, xi, zeta = decay_ref[hd], xi_ref[hd], zeta_ref[hd]
        g_chunk = jnp.exp(chunk * lg_ref[hd])
        q = q_ref[:, hd * RET_KEY_DIM:(hd + 1) * RET_KEY_DIM]
        k = k_ref[:, hd * RET_KEY_DIM:(hd + 1) * RET_KEY_DIM]
        v = v_ref[:, hd * RET_VAL_DIM:(hd + 1) * RET_VAL_DIM]
        state = state_ref[hd]
        sc = _dot_nt(q, k) * decay
        inner = _dot(sc.astype(BF16), v)
        cross = _dot(q, state.astype(BF16)) * _tile_lanes(xi, RET_VAL_DIM)
        kz = (k.astype(F32) * _tile_lanes(zeta, RET_KEY_DIM)).astype(BF16)
        upd = lax.dot_general(kz, v, (((0,), (0,)), ((), ())), preferred_element_type=F32)
        state_ref[hd] = state * _tile_lanes(g_chunk[0:1, :], RET_VAL_DIM) + upd
        o = inner + cross
        o = o * lax.rsqrt(jnp.mean(o * o, axis=-1, keepdims=True) + EPS)
        o = o * gn_ref[:, hd * RET_VAL_DIM:(hd + 1) * RET_VAL_DIM]
        g = g_ref[:, hd * RET_VAL_DIM:(hd + 1) * RET_VAL_DIM]
        o_ref[:, hd * RET_VAL_DIM:(hd + 1) * RET_VAL_DIM] = (g * (1.0 / (1.0 + jnp.exp(-g))) * o).astype(o_ref.dtype)


def _ret_core(q, k, v, g, gn_g):
    s = q.shape[0]
    chunk = RET_CHUNK
    hk, hv = RET_HEADS * RET_KEY_DIM, RET_HEADS * RET_VAL_DIM
    log_gamma = jnp.log(1.0 - 2.0 ** (-5.0 - jnp.arange(RET_HEADS, dtype=F32)))
    lg = jnp.broadcast_to(log_gamma[:, None, None], (RET_HEADS, SUBLANES, LANES))
    row = lambda w: pl.BlockSpec((chunk, w), lambda i: (i, 0))
    return pl.pallas_call(
        functools.partial(_ret_core_kernel, chunk=chunk),
        out_shape=jax.ShapeDtypeStruct((s, hv), BF16),
        grid=(s // chunk,),
        in_specs=[_full(lg.shape), row(hk), row(hk), row(hv), row(hv), _full((1, hv))],
        out_specs=row(hv),
        scratch_shapes=[pltpu.VMEM((RET_HEADS, RET_KEY_DIM, RET_VAL_DIM), F32),
                        pltpu.VMEM((RET_HEADS, chunk, chunk), F32), pltpu.VMEM((RET_HEADS, chunk, LANES), F32),
                        pltpu.VMEM((RET_HEADS, chunk, LANES), F32)],
        compiler_params=_params("arbitrary"),
        name="ret_core",
    )(lg, q, k, v, g, gn_g.reshape(1, hv))


def _post_kernel(o_ref, x_ref, mod_ref, ng_ref, wo_ref, w1_ref, w2_ref, out_ref, y_ref):
    x = x_ref[...]
    y = _dot(o_ref[...], wo_ref[...])
    x = x + mod_ref[2:3, :] * _rms(y, ng_ref[1:2, :])
    h = _modulated(x, ng_ref[2:3, :], mod_ref[3:4, :], mod_ref[4:5, :])
    d_ff = w1_ref.shape[1]
    for c in range(d_ff // FF_CHUNK):
        a = jnp.maximum(_dot(h, w1_ref[:, c * FF_CHUNK:(c + 1) * FF_CHUNK]), 0.0)
        part = _dot((a * a).astype(BF16), w2_ref[c * FF_CHUNK:(c + 1) * FF_CHUNK, :])
        if c == 0:
            y_ref[...] = part
        else:
            y_ref[...] += part
    out_ref[...] = x + mod_ref[5:6, :] * _rms(y_ref[...], ng_ref[3:4, :])


def _post(o, x, mod, ng, w_out, w1_all, w2_all, layer):
    s, d = x.shape
    tm = POST_ROW_TILE
    ko = o.shape[1]
    row = lambda w: pl.BlockSpec((tm, w), lambda i: (i, 0))
    resident = lambda a: pl.BlockSpec(a.shape, lambda i: (0, 0), pipeline_mode=pl.Buffered(1))
    of_layer = lambda a: pl.BlockSpec((None,) + a.shape[1:], lambda i: (layer, 0, 0), pipeline_mode=pl.Buffered(1))
    wo = w_out.astype(BF16)
    return pl.pallas_call(
        _post_kernel,
        out_shape=jax.ShapeDtypeStruct((s, d), F32),
        grid=(s // tm,),
        in_specs=[row(ko), row(d), _full(mod.shape), _full(ng.shape), resident(wo), of_layer(w1_all),
                  of_layer(w2_all)],
        out_specs=row(d),
        scratch_shapes=[pltpu.VMEM((tm, d), F32)],
        compiler_params=_params("arbitrary"),
        name="post",
    )(o, x, mod, ng, wo, w1_all, w2_all)


def kernel(x, c, positions, ada_w, ada_b, norm_g, ffn_w1, ffn_w2, mla_w_in, mla_q_norm, mla_kv_norm, mla_w_q_up,
           mla_w_kv_up, mla_w_out, sb_w_in, sb_w_out, ret_w_in, ret_gn_g, ret_w_out):
    batch, seq, d = x.shape
    assert batch == 1 and seq % MLA_Q_BLOCK == 0 and seq % ROW_TILE == 0 and seq % POST_ROW_TILE == 0
    depth = ada_w.shape[0]
    mod = _adaln(c, ada_w, ada_b)
    tables = _rope_tables(positions)
    w1_all, w2_all = ffn_w1.astype(BF16), ffn_w2.astype(BF16)
    xs = x.reshape(seq, d)
    for i in range(depth):
        kind, j = i % 3, i // 3
        if kind == 0:
            o = _mla_mixer(xs, mod[i], norm_g[i], tables, mla_w_in[j], mla_q_norm[j], mla_kv_norm[j],
                           mla_w_q_up[j], mla_w_kv_up[j])
            w_out = mla_w_out[j]
        elif kind == 1:
            q, k, v = _sb_in(xs, mod[i], norm_g[i], sb_w_in[j])
            o, w_out = _sb_attn(q, k, v), sb_w_out[j]
        else:
            q, k, v, g = _ret_in(xs, mod[i], norm_g[i], tables[2], tables[3], ret_w_in[j])
            o, w_out = _ret_core(q, k, v, g, ret_gn_g[j]), ret_w_out[j]
        xs = _post(o, xs, mod[i], norm_g[i], w_out, w1_all, w2_all, i)
    return xs.reshape(batch, seq, d)
```

```python
import functools
import math

import jax
import jax.numpy as jnp
from jax import lax
from jax.experimental import pallas as pl
from jax.experimental.pallas import tpu as pltpu

EPS = 1e-6
ROPE_BASE = 10000.0
MASK_VALUE = -1e30
LOG2E = math.log2(math.e)

LANES = 128
SUBLANES = 8
MXU_DEPTH = 256
VMEM_LIMIT_BYTES = 60000 * 1024

MLA_HEADS = 8
MLA_Q_RANK = 256
MLA_KV_RANK = 128
MLA_NOPE = 128
MLA_ROPE = 64
MLA_V = 128
MLA_QK_PAD = MXU_DEPTH

SB_HEADS = 16
SB_HEAD_DIM = 64

RET_HEADS = 4
RET_KEY_DIM = 256
RET_VAL_DIM = 512

ROW_TILE = 512
ADALN_COL_TILE = 768
ROPE_ROW_TILE = 1024
POST_ROW_TILE = 1024
FF_CHUNK = 1024
MLA_BLOCK = 512
MLA_Q_BLOCK = 2048
MLA_Q_GROUP = 256
SB_Q_BLOCK = 256
SB_K_BLOCK = 256
SB_HEADS_PER_STEP = 8
SB_GROUP_HEADS = MXU_DEPTH // SB_HEAD_DIM
SB_EXP2_CLAMP = 126.0
SB_EXP2_UNDERFLOW = 160.0
RET_CHUNK = 256

F32 = jnp.float32
BF16 = jnp.bfloat16


def _params(*semantics):
    return pltpu.CompilerParams(dimension_semantics=semantics, vmem_limit_bytes=VMEM_LIMIT_BYTES)


def _full(shape):
    nd = len(shape)
    return pl.BlockSpec(shape, lambda *_: (0,) * nd)


def _rms(x, g):
    return x * lax.rsqrt(jnp.mean(x * x, axis=-1, keepdims=True) + EPS) * g


def _dot(a, b):
    return jnp.dot(a, b, preferred_element_type=F32)


def _dot_nt(a, b):
    return lax.dot_general(a, b, (((1,), (1,)), ((), ())), preferred_element_type=F32)


def _tile_lanes(v, width):
    reps = width // LANES
    return v if reps == 1 else jnp.concatenate([v] * reps, axis=1)


def _adaln_kernel(c_ref, w_ref, b_ref, o_ref):
    c = c_ref[...]
    cond = c * (1.0 / (1.0 + jnp.exp(-c)))
    o_ref[0] = jnp.sum(w_ref[0] * cond, axis=0, keepdims=True) + b_ref[0]


def _adaln(c, ada_w, ada_b):
    depth, d, n = ada_w.shape
    tn = ADALN_COL_TILE
    out = pl.pallas_call(
        _adaln_kernel,
        out_shape=jax.ShapeDtypeStruct((depth, 1, n), F32),
        grid=(depth, n // tn),
        in_specs=[
            _full((d, 1)),
            pl.BlockSpec((1, d, tn), lambda i, j: (i, 0, j)),
            pl.BlockSpec((1, 1, tn), lambda i, j: (i, 0, j)),
        ],
        out_specs=pl.BlockSpec((1, 1, tn), lambda i, j: (i, 0, j)),
        compiler_params=_params("arbitrary", "arbitrary"),
        name="adaln",
    )(c.reshape(d, 1), ada_w, ada_b.reshape(depth, 1, n))
    return out.reshape(depth, 6, d)


def _rope_kernel(pos_ref, f64_ref, f256_ref, a_mask_ref, b_sign_ref, a_ref, b_ref, cos_ref, sin_ref):
    pos = pos_ref[...]
    ang = pos * f64_ref[...]
    a_ref[...] = jnp.cos(ang) * a_mask_ref[...]
    b_ref[...] = jnp.sin(ang) * b_sign_ref[...]
    ang = pos * f256_ref[...]
    cos_ref[...] = jnp.cos(ang)
    sin_ref[...] = jnp.sin(ang)


def _rope_tables(positions):
    s = positions.shape[-1]
    tm = ROPE_ROW_TILE
    pos = positions.astype(F32).reshape(s, 1)
    f64 = ROPE_BASE ** (-jnp.arange(0, MLA_ROPE, 2, dtype=F32) / MLA_ROPE)
    f256 = ROPE_BASE ** (-jnp.arange(0, RET_KEY_DIM, 2, dtype=F32) / RET_KEY_DIM)
    quarter = MLA_ROPE // 2
    ones, zeros = jnp.ones((quarter,), F32), jnp.zeros((quarter,), F32)
    a_mask = jnp.concatenate([ones, ones, zeros, zeros]).reshape(1, LANES)
    b_sign = jnp.concatenate([-ones, ones, zeros, zeros]).reshape(1, LANES)
    row = pl.BlockSpec((tm, LANES), lambda i: (i, 0))
    return pl.pallas_call(
        _rope_kernel,
        out_shape=[jax.ShapeDtypeStruct((s, LANES), F32)] * 4,
        grid=(s // tm,),
        in_specs=[pl.BlockSpec((tm, 1), lambda i: (i, 0))] + [_full((1, LANES))] * 4,
        out_specs=[row] * 4,
        compiler_params=_params("arbitrary"),
        name="rope_tables",
    )(pos, jnp.tile(f64, 4).reshape(1, LANES), f256.reshape(1, LANES), a_mask, b_sign)


def _modulated(x, g, shift, scale):
    return (_rms(x, g) * (1.0 + scale) + shift).astype(BF16)


def _mla_in_kernel(x_ref, mod_ref, ng_ref, a_ref, b_ref, w_in_ref, qn_ref, kvn_ref, wq_ref, wk_ref, wv_ref,
                   q_ref, k_ref, v_ref, *, q_scale):
    h = _modulated(x_ref[...], ng_ref[0:1, :], mod_ref[0:1, :], mod_ref[1:2, :])
    lat = _dot(h, w_in_ref[...])
    rope_a, rope_b = a_ref[...], b_ref[...]

    def rope(v):
        return v * rope_a + pltpu.roll(v, 2 * (MLA_ROPE // 2), axis=1) * rope_b

    q_lat = lat[:, :MLA_Q_RANK]
    kv_lat = lat[:, MLA_Q_RANK:MLA_Q_RANK + MLA_KV_RANK]
    k_pe = rope(lat[:, MLA_Q_RANK + MLA_KV_RANK:]).astype(BF16)
    q = _dot(_rms(q_lat, qn_ref[...]).astype(BF16), wq_ref[...])
    kvn = _rms(kv_lat, kvn_ref[...]).astype(BF16)
    k_nope = _dot(kvn, wk_ref[...]).astype(BF16)
    v_t = _dot_nt(wv_ref[...], kvn).astype(BF16)
    v_ref[...] = v_t.reshape(MLA_HEADS, 1, MLA_V, v_t.shape[1])
    q_parts, k_parts = [], []
    for hd in range(MLA_HEADS):
        lo = hd * MLA_QK_PAD
        q_parts.append((q[:, lo:lo + MLA_NOPE] * q_scale).astype(BF16))
        q_parts.append((rope(q[:, lo + MLA_NOPE:lo + MLA_QK_PAD]) * q_scale).astype(BF16))
        k_parts.append(k_nope[:, hd * MLA_NOPE:(hd + 1) * MLA_NOPE])
        k_parts.append(k_pe)
    q_ref[...] = jnp.concatenate(q_parts, axis=1)
    k_ref[...] = jnp.concatenate(k_parts, axis=1)


def _mla_weights(w_in, w_q_up, w_kv_up):
    half = MLA_ROPE // 2
    pe = w_in[:, MLA_Q_RANK + MLA_KV_RANK:]
    pe1, pe2 = pe[:, :half], pe[:, half:]
    w_in_p = jnp.concatenate([w_in[:, :MLA_Q_RANK + MLA_KV_RANK], pe1, pe2, pe2, pe1], axis=1)
    wq = w_q_up.reshape(MLA_Q_RANK, MLA_HEADS, MLA_NOPE + MLA_ROPE)
    q1, q2 = wq[..., MLA_NOPE:MLA_NOPE + half], wq[..., MLA_NOPE + half:]
    wq_p = jnp.concatenate([wq[..., :MLA_NOPE], q1, q2, q2, q1], axis=-1).reshape(MLA_Q_RANK, MLA_HEADS * MLA_QK_PAD)
    wkv = w_kv_up.reshape(MLA_KV_RANK, MLA_HEADS, MLA_NOPE + MLA_V)
    wk = wkv[..., :MLA_NOPE].reshape(MLA_KV_RANK, MLA_HEADS * MLA_NOPE)
    wv_t = wkv[..., MLA_NOPE:].reshape(MLA_KV_RANK, MLA_HEADS * MLA_V).T
    return w_in_p.astype(BF16), wq_p.astype(BF16), wk.astype(BF16), wv_t.astype(BF16)


def _mla_in(x, mod, ng, rope_a, rope_b, w_in, q_norm, kv_norm, w_q_up, w_kv_up):
    s, d = x.shape
    tm = MLA_BLOCK
    w_in_p, wq_p, wk, wv_t = _mla_weights(w_in, w_q_up, w_kv_up)
    q_scale = (MLA_NOPE + MLA_ROPE) ** -0.5 * LOG2E
    row = lambda n: pl.BlockSpec((tm, n), lambda i: (i, 0))
    hq = MLA_HEADS * MLA_QK_PAD
    return pl.pallas_call(
        functools.partial(_mla_in_kernel, q_scale=q_scale),
        out_shape=[jax.ShapeDtypeStruct((s, hq), BF16), jax.ShapeDtypeStruct((s, hq), BF16),
                   jax.ShapeDtypeStruct((MLA_HEADS, s // tm, MLA_V, tm), BF16)],
        grid=(s // tm,),
        in_specs=[row(d), _full(mod.shape), _full(ng.shape), row(LANES), row(LANES), _full(w_in_p.shape),
                  _full((1, MLA_Q_RANK)), _full((1, MLA_KV_RANK)), _full(wq_p.shape), _full(wk.shape),
                  _full(wv_t.shape)],
        out_specs=[row(hq), row(hq), pl.BlockSpec((MLA_HEADS, 1, MLA_V, tm), lambda i: (0, i, 0, 0))],
        compiler_params=_params("arbitrary"),
        name="mla_in",
    )(x, mod, ng, rope_a, rope_b, w_in_p, q_norm.reshape(1, -1), kv_norm.reshape(1, -1), wq_p, wk, wv_t)


def _interleave(producers, consumers, lead=2):
    order = list(producers[:lead])
    rest = list(producers[lead:])
    for f in consumers:
        order.append(f)
        if rest:
            order.append(rest.pop(0))
    return order + rest


def _mla_attn_kernel(q_ref, k_ref, v_ref, o_ref, *scratch, bq, bk, sub):
    i = pl.program_id(1)
    groups = bq // sub
    ratio = bq // bk
    assert bq == ratio * bk and ratio % 2 == 0
    m_refs, l_refs = scratch[:groups], scratch[groups:2 * groups]
    acc_refs = scratch[2 * groups:3 * groups]
    s_refs = (scratch[3 * groups:4 * groups], scratch[4 * groups:5 * groups])
    qt_ref = scratch[5 * groups]
    qt_ref[...] = q_ref[...].astype(F32).T.astype(BF16)
    for c in range(groups):
        m_refs[c][...] = jnp.full((1, sub), MASK_VALUE, F32)
        l_refs[c][...] = jnp.zeros((1, sub), F32)
        acc_refs[c][...] = jnp.zeros((MLA_V, sub), F32)

    def offset_of(c, diag):
        return None if diag is None else c * sub - diag * bk

    def live_groups(diag):
        return [c for c in range(groups) if diag is None or offset_of(c, diag) >= -(sub - 1)]

    def producers(slot, j, diag):
        start = pl.multiple_of(j * bk, bk)
        k = k_ref[pl.ds(start, bk), :]

        def make(c):
            def produce():
                s_refs[slot][c][...] = _dot(k, qt_ref[:, c * sub:(c + 1) * sub])
            return produce
        return [make(c) for c in live_groups(diag)]

    def consumers(slot, j, diag):
        v = v_ref[0, j]

        def make(c):
            offset = offset_of(c, diag)

            def consume():
                s = s_refs[slot][c][...]
                if offset is not None and offset < bk - 1:
                    key = lax.broadcasted_iota(jnp.int32, (bk, sub), 0)
                    qry = lax.broadcasted_iota(jnp.int32, (bk, sub), 1)
                    s = jnp.where(key - qry <= offset, s, MASK_VALUE)
                m_prev = m_refs[c][...]
                m_new = jnp.maximum(m_prev, jnp.max(s, axis=0, keepdims=True))
                alpha = jnp.exp2(m_prev - m_new)
                p = jnp.exp2(s - m_new)
                l_refs[c][...] = alpha * l_refs[c][...] + jnp.sum(p, axis=0, keepdims=True)
                m_refs[c][...] = m_new
                acc_refs[c][...] = alpha * acc_refs[c][...] + _dot(v, p.astype(BF16))
            return consume
        return [make(c) for c in live_groups(diag)]

    def stage(slot, j, diag, next_j, next_diag):
        nxt = [] if next_j is None else producers(1 - slot, next_j, next_diag)
        for f in _interleave(nxt, consumers(slot, j, diag)):
            f()

    for f in producers(0, 0, None):
        f()

    def body(t, carry):
        for u in range(ratio):
            j = ratio * t + u
            stage(u % 2, j, None, j + 1, None)
        return carry

    first_diag = i * ratio
    lax.fori_loop(0, i, body, 0)
    for dj in range(ratio):
        last = dj == ratio - 1
        stage(dj % 2, first_diag + dj, dj, None if last else first_diag + dj + 1, None if last else dj + 1)
    for c in range(groups):
        o_ref[c * sub:(c + 1) * sub, :] = (acc_refs[c][...] / l_refs[c][...]).T.astype(o_ref.dtype)


def _mla_attn(q, k, v_t):
    s = q.shape[0]
    bq, bk, sub = MLA_Q_BLOCK, MLA_BLOCK, MLA_Q_GROUP
    groups = bq // sub
    return pl.pallas_call(
        functools.partial(_mla_attn_kernel, bq=bq, bk=bk, sub=sub),
        out_shape=jax.ShapeDtypeStruct((s, MLA_HEADS * MLA_V), BF16),
        grid=(MLA_HEADS, s // bq),
        in_specs=[pl.BlockSpec((bq, MLA_QK_PAD), lambda h, i: (i, h)),
                  pl.BlockSpec((s, MLA_QK_PAD), lambda h, i: (0, h)),
                  pl.BlockSpec((1, s // bk, MLA_V, bk), lambda h, i: (h, 0, 0, 0))],
        out_specs=pl.BlockSpec((bq, MLA_V), lambda h, i: (i, h)),
        scratch_shapes=([pltpu.VMEM((1, sub), F32)] * (2 * groups) + [pltpu.VMEM((MLA_V, sub), F32)] * groups
                        + [pltpu.VMEM((bk, sub), F32)] * (2 * groups) + [pltpu.VMEM((MLA_QK_PAD, bq), BF16)]),
        compiler_params=_params("arbitrary", "arbitrary"),
        name="mla_attn",
    )(q, k, v_t)


def _mla_mixer(x, mod, ng, tables, w_in, q_norm, kv_norm, w_q_up, w_kv_up):
    q, k, v = _mla_in(x, mod, ng, tables[0], tables[1], w_in, q_norm, kv_norm, w_q_up, w_kv_up)
    return _mla_attn(q, k, v)


def _sb_in_kernel(x_ref, mod_ref, ng_ref, wqv_ref, wkt_ref, q_ref, kt_ref, v_ref, *, q_scale, bk):
    h = _modulated(x_ref[...], ng_ref[0:1, :], mod_ref[0:1, :], mod_ref[1:2, :])
    qv = _dot(h, wqv_ref[...])
    n = SB_HEADS * SB_HEAD_DIM
    q_ref[...] = (qv[:, :n] * q_scale).astype(BF16)
    v_ref[...] = qv[:, n:].astype(BF16)
    k_t = _dot_nt(wkt_ref[...], h).astype(BF16)
    for b in range(k_t.shape[1] // bk):
        kt_ref[b] = k_t[:, b * bk:(b + 1) * bk]


def _sb_in(x, mod, ng, w_in):
    s, d = x.shape
    tm, bk = ROW_TILE, SB_K_BLOCK
    n = SB_HEADS * SB_HEAD_DIM
    row = lambda w: pl.BlockSpec((tm, w), lambda i: (i, 0))
    w = w_in.astype(BF16)
    w_qv = jnp.concatenate([w[:, :n], w[:, 2 * n:]], axis=1)
    wk_t = w[:, n:2 * n].T
    return pl.pallas_call(
        functools.partial(_sb_in_kernel, q_scale=-LOG2E * SB_HEAD_DIM ** -0.5, bk=bk),
        out_shape=[jax.ShapeDtypeStruct((s, n), BF16), jax.ShapeDtypeStruct((s // bk, n, bk), BF16),
                   jax.ShapeDtypeStruct((s, n), BF16)],
        grid=(s // tm,),
        in_specs=[row(d), _full(mod.shape), _full(ng.shape), _full(w_qv.shape), _full(wk_t.shape)],
        out_specs=[row(n), pl.BlockSpec((tm // bk, n, bk), lambda i: (i, 0, 0)), row(n)],
        compiler_params=_params("arbitrary"),
        name="sb_in",
    )(x, mod, ng, w_qv, wk_t)


def _sb_attn_kernel(q_ref, kt_ref, v_ref, tri_ref, o_ref, acc_ref, *scratch, bq, bk, heads):
    i = pl.program_id(1)
    ratio = bq // bk
    assert bq == ratio * bk and heads % SB_GROUP_HEADS == 0
    gw = SB_GROUP_HEADS * SB_HEAD_DIM
    c_refs = scratch[:heads]
    w_refs = (scratch[heads:2 * heads], scratch[2 * heads:3 * heads])
    lane_q = lax.broadcasted_iota(jnp.int32, (bq, gw), 1)
    lane_v = lax.broadcasted_iota(jnp.int32, (bk, gw), 1)

    def group_of(hd):
        return slice((hd // SB_GROUP_HEADS) * gw, (hd // SB_GROUP_HEADS + 1) * gw)

    def head_lanes(lane, hd):
        lo = (hd % SB_GROUP_HEADS) * SB_HEAD_DIM
        return (lane >= lo) & (lane < lo + SB_HEAD_DIM)

    def only_head(x, lane, hd):
        return jnp.where(head_lanes(lane, hd), x, jnp.zeros_like(x))

    q_heads = [only_head(q_ref[:, group_of(hd)], lane_q, hd) for hd in range(heads)]
    acc_ref[...] = jnp.zeros(acc_ref.shape, F32)
    for hd in range(heads):
        c_refs[hd][...] = jnp.zeros((bq, LANES), F32)
    tri = tri_ref[...]

    def producers(slot, j):
        k_t = kt_ref[j]

        def make(hd):
            def produce():
                w_refs[slot][hd][...] = _dot(q_heads[hd], k_t[group_of(hd), :])
            return produce
        return [make(hd) for hd in range(heads)]

    def stage(slot, j, mask_offset, next_j):
        start = pl.multiple_of(j * bk, bk)
        v = v_ref[pl.ds(start, bk), :]
        if mask_offset is not None:
            row = lax.broadcasted_iota(jnp.int32, (bq, bk), 0)
            col = lax.broadcasted_iota(jnp.int32, (bq, bk), 1)
            strict = (col - row) < mask_offset
        weights = [None] * heads

        def make(hd):
            def consume():
                w = w_refs[slot][hd][...]
                neg_log_b = jnp.log2(1.0 + jnp.exp2(jnp.minimum(w, SB_EXP2_CLAMP)))
                log_1m = jnp.minimum(w - neg_log_b, 0.0)
                if mask_offset is not None:
                    log_1m = jnp.where(strict, log_1m, 0.0)
                suffix = _dot(log_1m.astype(BF16), tri)
                c = c_refs[hd][...]
                a = jnp.exp2(suffix - neg_log_b + _tile_lanes(c, bk))
                if mask_offset is not None:
                    a = jnp.where(strict, a, 0.0)
                weights[hd] = a.astype(BF16)
                c_refs[hd][...] = c + jnp.sum(log_1m, axis=1, keepdims=True)
            return consume

        for f in _interleave(producers(1 - slot, next_j), [make(hd) for hd in range(heads)]):
            f()
        for g in range(heads // SB_GROUP_HEADS):
            members = range(g * SB_GROUP_HEADS, (g + 1) * SB_GROUP_HEADS)
            v_heads = [only_head(v[:, group_of(hd)], lane_v, hd) for hd in members]
            acc_ref[:, g * gw:(g + 1) * gw] += _dot(jnp.concatenate([weights[hd] for hd in members], axis=1),
                                                    jnp.concatenate(v_heads, axis=0))

    last = ratio * i + ratio - 1
    for f in producers(0, last):
        f()
    for p in range(ratio):
        dj = ratio - 1 - p
        stage(p % 2, ratio * i + dj, -dj * bk, jnp.maximum(ratio * i + dj - 1, 0))

    def any_weight_left():
        worst = c_refs[0][...]
        for hd in range(1, heads):
            worst = jnp.maximum(worst, c_refs[hd][...])
        return (jnp.max(worst) > -SB_EXP2_UNDERFLOW).astype(jnp.int32)

    def more(carry):
        n, left = carry
        return jnp.logical_and(n < ratio * i, left > 0)

    def body(carry):
        n, _ = carry
        j = ratio * i - 1 - n
        nxt = jnp.maximum(j - 1, 0)
        for slot in range(2):
            @pl.when((n + ratio) % 2 == slot)
            def _():
                stage(slot, j, None, nxt)
        return n + 1, any_weight_left()

    lax.while_loop(more, body, (0, any_weight_left()))
    o_ref[...] = acc_ref[...].astype(o_ref.dtype)


def _sb_attn(q, k_t, v):
    s = q.shape[0]
    bq, bk, heads = SB_Q_BLOCK, SB_K_BLOCK, SB_HEADS_PER_STEP
    width = heads * SB_HEAD_DIM
    tri = jnp.tril(jnp.ones((bk, bk), F32), k=-1).astype(BF16)
    return pl.pallas_call(
        functools.partial(_sb_attn_kernel, bq=bq, bk=bk, heads=heads),
        out_shape=jax.ShapeDtypeStruct((s, SB_HEADS * SB_HEAD_DIM), BF16),
        grid=(SB_HEADS // heads, s // bq),
        in_specs=[pl.BlockSpec((bq, width), lambda p, i: (i, p)),
                  pl.BlockSpec((s // bk, width, bk), lambda p, i: (0, p, 0), pipeline_mode=pl.Buffered(1)),
                  pl.BlockSpec((s, width), lambda p, i: (0, p), pipeline_mode=pl.Buffered(1)), _full((bk, bk))],
        out_specs=pl.BlockSpec((bq, width), lambda p, i: (i, p)),
        scratch_shapes=([pltpu.VMEM((bq, width), F32)] + [pltpu.VMEM((bq, LANES), F32)] * heads
                        + [pltpu.VMEM((bq, bk), F32)] * (2 * heads)),
        compiler_params=_params("arbitrary", "arbitrary"),
        name="sb_attn",
    )(q, k_t, v, tri)


def _ret_in_kernel(x_ref, mod_ref, ng_ref, cos_ref, sin_ref, w_ref, q_ref, k_ref, v_ref, g_ref, *, k_scale):
    h = _modulated(x_ref[...], ng_ref[0:1, :], mod_ref[0:1, :], mod_ref[1:2, :])
    y = _dot(h, w_ref[...])
    hk, hv = RET_HEADS * RET_KEY_DIM, RET_HEADS * RET_VAL_DIM
    cos, sin = cos_ref[...], sin_ref[...]
    half = RET_KEY_DIM // 2

    def rope(t, scale):
        parts = []
        for hd in range(RET_HEADS):
            t1 = t[:, hd * RET_KEY_DIM:hd * RET_KEY_DIM + half]
            t2 = t[:, hd * RET_KEY_DIM + half:(hd + 1) * RET_KEY_DIM]
            parts.append(((t1 * cos - t2 * sin) * scale).astype(BF16))
            parts.append(((t2 * cos + t1 * sin) * scale).astype(BF16))
        return jnp.concatenate(parts, axis=1)

    q_ref[...] = rope(y[:, :hk], 1.0)
    k_ref[...] = rope(y[:, hk:2 * hk], k_scale)
    v_ref[...] = y[:, 2 * hk:2 * hk + hv].astype(BF16)
    g_ref[...] = y[:, 2 * hk + hv:]


def _ret_in(x, mod, ng, cos, sin, w_in):
    s, d = x.shape
    tm = ROW_TILE
    hk, hv = RET_HEADS * RET_KEY_DIM, RET_HEADS * RET_VAL_DIM
    row = lambda w: pl.BlockSpec((tm, w), lambda i: (i, 0))
    w = w_in.astype(BF16)
    return pl.pallas_call(
        functools.partial(_ret_in_kernel, k_scale=RET_KEY_DIM ** -0.5),
        out_shape=[jax.ShapeDtypeStruct((s, hk), BF16), jax.ShapeDtypeStruct((s, hk), BF16),
                   jax.ShapeDtypeStruct((s, hv), BF16), jax.ShapeDtypeStruct((s, hv), F32)],
        grid=(s // tm,),
        in_specs=[row(d), _full(mod.shape), _full(ng.shape), row(LANES), row(LANES), _full(w.shape)],
        out_specs=[row(hk), row(hk), row(hv), row(hv)],
        compiler_params=_params("arbitrary"),
        name="ret_in",
    )(x, mod, ng, cos, sin, w)


def _ret_core_kernel(lg_ref, q_ref, k_ref, v_ref, g_ref, gn_ref, o_ref, state_ref, decay_ref, xi_ref, zeta_ref, *,
                     chunk):
    @pl.when(pl.program_id(0) == 0)
    def _():
        state_ref[...] = jnp.zeros(state_ref.shape, F32)
        row = lax.broadcasted_iota(jnp.int32, (chunk, chunk), 0)
        col = lax.broadcasted_iota(jnp.int32, (chunk, chunk), 1)
        diff = (row - col).astype(F32)
        idx = lax.broadcasted_iota(jnp.int32, (chunk, LANES), 0).astype(F32)
        for hd in range(RET_HEADS):
            lg_row = lg_ref[hd][0:1, :]
            decay_ref[hd] = jnp.where(diff >= 0.0,
                                      jnp.exp(jnp.maximum(diff, 0.0) * _tile_lanes(lg_row, chunk)), 0.0)
            xi_ref[hd] = jnp.exp((idx + 1.0) * lg_row)
            zeta_ref[hd] = jnp.exp((chunk - 1.0 - idx) * lg_row)

    for hd in range(RET_HEADS):
        decay, xi, zeta = decay_ref[hd], xi_ref[hd], zeta_ref[hd]
        g_chunk = jnp.exp(chunk * lg_ref[hd])
        q = q_ref[:, hd * RET_KEY_DIM:(hd + 1) * RET_KEY_DIM]
        k = k_ref[:, hd * RET_KEY_DIM:(hd + 1) * RET_KEY_DIM]
        v = v_ref[:, hd * RET_VAL_DIM:(hd + 1) * RET_VAL_DIM]
        state = state_ref[hd]
        sc = _dot_nt(q, k) * decay
        inner = _dot(sc.astype(BF16), v)
        cross = _dot(q, state.astype(BF16)) * _tile_lanes(xi, RET_VAL_DIM)
        kz = (k.astype(F32) * _tile_lanes(zeta, RET_KEY_DIM)).astype(BF16)
        upd = lax.dot_general(kz, v, (((0,), (0,)), ((), ())), preferred_element_type=F32)
        state_ref[hd] = state * _tile_lanes(g_chunk[0:1, :], RET_VAL_DIM) + upd
        o = inner + cross
        o = o * lax.rsqrt(jnp.mean(o * o, axis=-1, keepdims=True) + EPS)
        o = o * gn_ref[:, hd * RET_VAL_DIM:(hd + 1) * RET_VAL_DIM]
        g = g_ref[:, hd * RET_VAL_DIM:(hd + 1) * RET_VAL_DIM]
        o_ref[:, hd * RET_VAL_DIM:(hd + 1) * RET_VAL_DIM] = (g * (1.0 / (1.0 + jnp.exp(-g))) * o).astype(o_ref.dtype)


def _ret_core(q, k, v, g, gn_g):
    s = q.shape[0]
    chunk = RET_CHUNK
    hk, hv = RET_HEADS * RET_KEY_DIM, RET_HEADS * RET_VAL_DIM
    log_gamma = jnp.log(1.0 - 2.0 ** (-5.0 - jnp.arange(RET_HEADS, dtype=F32)))
    lg = jnp.broadcast_to(log_gamma[:, None, None], (RET_HEADS, SUBLANES, LANES))
    row = lambda w: pl.BlockSpec((chunk, w), lambda i: (i, 0))
    return pl.pallas_call(
        functools.partial(_ret_core_kernel, chunk=chunk),
        out_shape=jax.ShapeDtypeStruct((s, hv), BF16),
        grid=(s // chunk,),
        in_specs=[_full(lg.shape), row(hk), row(hk), row(hv), row(hv), _full((1, hv))],
        out_specs=row(hv),
        scratch_shapes=[pltpu.VMEM((RET_HEADS, RET_KEY_DIM, RET_VAL_DIM), F32),
                        pltpu.VMEM((RET_HEADS, chunk, chunk), F32), pltpu.VMEM((RET_HEADS, chunk, LANES), F32),
                        pltpu.VMEM((RET_HEADS, chunk, LANES), F32)],
        compiler_params=_params("arbitrary"),
        name="ret_core",
    )(lg, q, k, v, g, gn_g.reshape(1, hv))


def _post_kernel(o_ref, x_ref, mod_ref, ng_ref, wo_ref, w1_ref, w2_ref, out_ref, y_ref):
    x = x_ref[...]
    y = _dot(o_ref[...], wo_ref[...])
    x = x + mod_ref[2:3, :] * _rms(y, ng_ref[1:2, :])
    h = _modulated(x, ng_ref[2:3, :], mod_ref[3:4, :], mod_ref[4:5, :])
    d_ff = w1_ref.shape[1]
    for c in range(d_ff // FF_CHUNK):
        a = jnp.maximum(_dot(h, w1_ref[:, c * FF_CHUNK:(c + 1) * FF_CHUNK]), 0.0)
        part = _dot((a * a).astype(BF16), w2_ref[c * FF_CHUNK:(c + 1) * FF_CHUNK, :])
        if c == 0:
            y_ref[...] = part
        else:
            y_ref[...] += part
    out_ref[...] = x + mod_ref[5:6, :] * _rms(y_ref[...], ng_ref[3:4, :])


def _post(o, x, mod, ng, w_out, w1_all, w2_all, layer):
    s, d = x.shape
    tm = POST_ROW_TILE
    ko = o.shape[1]
    row = lambda w: pl.BlockSpec((tm, w), lambda i: (i, 0))
    resident = lambda a: pl.BlockSpec(a.shape, lambda i: (0, 0), pipeline_mode=pl.Buffered(1))
    of_layer = lambda a: pl.BlockSpec((None,) + a.shape[1:], lambda i: (layer, 0, 0), pipeline_mode=pl.Buffered(1))
    wo = w_out.astype(BF16)
    return pl.pallas_call(
        _post_kernel,
        out_shape=jax.ShapeDtypeStruct((s, d), F32),
        grid=(s // tm,),
        in_specs=[row(ko), row(d), _full(mod.shape), _full(ng.shape), resident(wo), of_layer(w1_all),
                  of_layer(w2_all)],
        out_specs=row(d),
        scratch_shapes=[pltpu.VMEM((tm, d), F32)],
        compiler_params=_params("arbitrary"),
        name="post",
    )(o, x, mod, ng, wo, w1_all, w2_all)


def kernel(x, c, positions, ada_w, ada_b, norm_g, ffn_w1, ffn_w2, mla_w_in, mla_q_norm, mla_kv_norm, mla_w_q_up,
           mla_w_kv_up, mla_w_out, sb_w_in, sb_w_out, ret_w_in, ret_gn_g, ret_w_out):
    batch, seq, d = x.shape
    assert batch == 1 and seq % MLA_Q_BLOCK == 0 and seq % ROW_TILE == 0 and seq % POST_ROW_TILE == 0
    depth = ada_w.shape[0]
    mod = _adaln(c, ada_w, ada_b)
    tables = _rope_tables(positions)
    w1_all, w2_all = ffn_w1.astype(BF16), ffn_w2.astype(BF16)
    xs = x.reshape(seq, d)
    for i in range(depth):
        kind, j = i % 3, i // 3
        if kind == 0:
            o = _mla_mixer(xs, mod[i], norm_g[i], tables, mla_w_in[j], mla_q_norm[j], mla_kv_norm[j],
                           mla_w_q_up[j], mla_w_kv_up[j])
            w_out = mla_w_out[j]
        elif kind == 1:
            q, k, v = _sb_in(xs, mod[i], norm_g[i], sb_w_in[j])
            o, w_out = _sb_attn(q, k, v), sb_w_out[j]
        else:
            q, k, v, g = _ret_in(xs, mod[i], norm_g[i], tables[2], tables[3], ret_w_in[j])
            o, w_out = _ret_core(q, k, v, g, ret_gn_g[j]), ret_w_out[j]
        xs = _post(o, xs, mod[i], norm_g[i], w_out, w1_all, w2_all, i)
    return xs.reshape(batch, seq, d)
```

```python
import functools
import math

import jax
import jax.numpy as jnp
from jax import lax
from jax.experimental import pallas as pl
from jax.experimental.pallas import tpu as pltpu

EPS = 1e-6
ROPE_BASE = 10000.0
MASK_VALUE = -1e30
LOG2E = math.log2(math.e)

LANES = 128
SUBLANES = 8
MXU_DEPTH = 256
VMEM_LIMIT_BYTES = 60000 * 1024

MLA_HEADS = 8
MLA_Q_RANK = 256
MLA_KV_RANK = 128
MLA_NOPE = 128
MLA_ROPE = 64
MLA_V = 128
MLA_V_ROWS = MLA_V + 16
MLA_QK_PAD = MXU_DEPTH

SB_HEADS = 16
SB_HEAD_DIM = 64

RET_HEADS = 4
RET_KEY_DIM = 256
RET_VAL_DIM = 512

ROW_TILE = 512
ADALN_COL_TILE = 768
ROPE_ROW_TILE = 1024
POST_ROW_TILE = 1024
FF_CHUNK = 1024
MLA_BLOCK = 512
MLA_Q_BLOCK = 2048
MLA_Q_GROUP = 256
SB_Q_BLOCK = 256
SB_K_BLOCK = 256
SB_HEADS_PER_STEP = 8
SB_GROUP_HEADS = MXU_DEPTH // SB_HEAD_DIM
SB_EXP2_CLAMP = 126.0
SB_EXP2_UNDERFLOW = 160.0
RET_CHUNK = 256

F32 = jnp.float32
BF16 = jnp.bfloat16


def _params(*semantics):
    return pltpu.CompilerParams(dimension_semantics=semantics, vmem_limit_bytes=VMEM_LIMIT_BYTES)


def _full(shape):
    nd = len(shape)
    return pl.BlockSpec(shape, lambda *_: (0,) * nd)


def _rms(x, g):
    return x * lax.rsqrt(jnp.mean(x * x, axis=-1, keepdims=True) + EPS) * g


def _dot(a, b):
    return jnp.dot(a, b, preferred_element_type=F32)


def _dot_nt(a, b):
    return lax.dot_general(a, b, (((1,), (1,)), ((), ())), preferred_element_type=F32)


def _tile_lanes(v, width):
    reps = width // LANES
    return v if reps == 1 else jnp.concatenate([v] * reps, axis=1)


def _adaln_kernel(c_ref, w_ref, b_ref, o_ref):
    c = c_ref[...]
    cond = c * (1.0 / (1.0 + jnp.exp(-c)))
    o_ref[0] = jnp.sum(w_ref[0] * cond, axis=0, keepdims=True) + b_ref[0]


def _adaln(c, ada_w, ada_b):
    depth, d, n = ada_w.shape
    tn = ADALN_COL_TILE
    out = pl.pallas_call(
        _adaln_kernel,
        out_shape=jax.ShapeDtypeStruct((depth, 1, n), F32),
        grid=(depth, n // tn),
        in_specs=[
            _full((d, 1)),
            pl.BlockSpec((1, d, tn), lambda i, j: (i, 0, j)),
            pl.BlockSpec((1, 1, tn), lambda i, j: (i, 0, j)),
        ],
        out_specs=pl.BlockSpec((1, 1, tn), lambda i, j: (i, 0, j)),
        compiler_params=_params("arbitrary", "arbitrary"),
        name="adaln",
    )(c.reshape(d, 1), ada_w, ada_b.reshape(depth, 1, n))
    return out.reshape(depth, 6, d)


def _rope_kernel(pos_ref, f64_ref, f256_ref, a_mask_ref, b_sign_ref, a_ref, b_ref, cos_ref, sin_ref):
    pos = pos_ref[...]
    ang = pos * f64_ref[...]
    a_ref[...] = jnp.cos(ang) * a_mask_ref[...]
    b_ref[...] = jnp.sin(ang) * b_sign_ref[...]
    ang = pos * f256_ref[...]
    cos_ref[...] = jnp.cos(ang)
    sin_ref[...] = jnp.sin(ang)


def _rope_tables(positions):
    s = positions.shape[-1]
    tm = ROPE_ROW_TILE
    pos = positions.astype(F32).reshape(s, 1)
    f64 = ROPE_BASE ** (-jnp.arange(0, MLA_ROPE, 2, dtype=F32) / MLA_ROPE)
    f256 = ROPE_BASE ** (-jnp.arange(0, RET_KEY_DIM, 2, dtype=F32) / RET_KEY_DIM)
    quarter = MLA_ROPE // 2
    ones, zeros = jnp.ones((quarter,), F32), jnp.zeros((quarter,), F32)
    a_mask = jnp.concatenate([ones, ones, zeros, zeros]).reshape(1, LANES)
    b_sign = jnp.concatenate([-ones, ones, zeros, zeros]).reshape(1, LANES)
    row = pl.BlockSpec((tm, LANES), lambda i: (i, 0))
    return pl.pallas_call(
        _rope_kernel,
        out_shape=[jax.ShapeDtypeStruct((s, LANES), F32)] * 4,
        grid=(s // tm,),
        in_specs=[pl.BlockSpec((tm, 1), lambda i: (i, 0))] + [_full((1, LANES))] * 4,
        out_specs=[row] * 4,
        compiler_params=_params("arbitrary"),
        name="rope_tables",
    )(pos, jnp.tile(f64, 4).reshape(1, LANES), f256.reshape(1, LANES), a_mask, b_sign)


def _modulated(x, g, shift, scale):
    return (_rms(x, g) * (1.0 + scale) + shift).astype(BF16)


def _mla_in_kernel(x_ref, mod_ref, ng_ref, a_ref, b_ref, w_in_ref, qn_ref, kvn_ref, wq_ref, wk_ref, wv_ref,
                   q_ref, k_ref, v_ref, *, q_scale):
    h = _modulated(x_ref[...], ng_ref[0:1, :], mod_ref[0:1, :], mod_ref[1:2, :])
    lat = _dot(h, w_in_ref[...])
    rope_a, rope_b = a_ref[...], b_ref[...]

    def rope(v):
        return v * rope_a + pltpu.roll(v, 2 * (MLA_ROPE // 2), axis=1) * rope_b

    q_lat = lat[:, :MLA_Q_RANK]
    kv_lat = lat[:, MLA_Q_RANK:MLA_Q_RANK + MLA_KV_RANK]
    k_pe = rope(lat[:, MLA_Q_RANK + MLA_KV_RANK:]).astype(BF16)
    q = _dot(_rms(q_lat, qn_ref[...]).astype(BF16), wq_ref[...])
    kvn = _rms(kv_lat, kvn_ref[...]).astype(BF16)
    k_nope = _dot(kvn, wk_ref[...]).astype(BF16)
    v_t = _dot_nt(wv_ref[...], kvn).astype(BF16)
    v_t = v_t.reshape(MLA_HEADS, 1, MLA_V, v_t.shape[1])
    ones = jnp.ones((MLA_HEADS, 1, MLA_V_ROWS - MLA_V, v_t.shape[3]), BF16)
    v_ref[...] = jnp.concatenate([v_t, ones], axis=2)
    q_parts, k_parts = [], []
    for hd in range(MLA_HEADS):
        lo = hd * MLA_QK_PAD
        q_parts.append((q[:, lo:lo + MLA_NOPE] * q_scale).astype(BF16))
        q_parts.append((rope(q[:, lo + MLA_NOPE:lo + MLA_QK_PAD]) * q_scale).astype(BF16))
        k_parts.append(k_nope[:, hd * MLA_NOPE:(hd + 1) * MLA_NOPE])
        k_parts.append(k_pe)
    q_ref[...] = jnp.concatenate(q_parts, axis=1)
    k_ref[...] = jnp.concatenate(k_parts, axis=1)


def _mla_weights(w_in, w_q_up, w_kv_up):
    half = MLA_ROPE // 2
    pe = w_in[:, MLA_Q_RANK + MLA_KV_RANK:]
    pe1, pe2 = pe[:, :half], pe[:, half:]
    w_in_p = jnp.concatenate([w_in[:, :MLA_Q_RANK + MLA_KV_RANK], pe1, pe2, pe2, pe1], axis=1)
    wq = w_q_up.reshape(MLA_Q_RANK, MLA_HEADS, MLA_NOPE + MLA_ROPE)
    q1, q2 = wq[..., MLA_NOPE:MLA_NOPE + half], wq[..., MLA_NOPE + half:]
    wq_p = jnp.concatenate([wq[..., :MLA_NOPE], q1, q2, q2, q1], axis=-1).reshape(MLA_Q_RANK, MLA_HEADS * MLA_QK_PAD)
    wkv = w_kv_up.reshape(MLA_KV_RANK, MLA_HEADS, MLA_NOPE + MLA_V)
    wk = wkv[..., :MLA_NOPE].reshape(MLA_KV_RANK, MLA_HEADS * MLA_NOPE)
    wv_t = wkv[..., MLA_NOPE:].reshape(MLA_KV_RANK, MLA_HEADS * MLA_V).T
    return w_in_p.astype(BF16), wq_p.astype(BF16), wk.astype(BF16), wv_t.astype(BF16)


def _mla_in(x, mod, ng, rope_a, rope_b, w_in, q_norm, kv_norm, w_q_up, w_kv_up):
    s, d = x.shape
    tm = MLA_BLOCK
    w_in_p, wq_p, wk, wv_t = _mla_weights(w_in, w_q_up, w_kv_up)
    q_scale = (MLA_NOPE + MLA_ROPE) ** -0.5 * LOG2E
    row = lambda n: pl.BlockSpec((tm, n), lambda i: (i, 0))
    hq = MLA_HEADS * MLA_QK_PAD
    return pl.pallas_call(
        functools.partial(_mla_in_kernel, q_scale=q_scale),
        out_shape=[jax.ShapeDtypeStruct((s, hq), BF16), jax.ShapeDtypeStruct((s, hq), BF16),
                   jax.ShapeDtypeStruct((MLA_HEADS, s // tm, MLA_V_ROWS, tm), BF16)],
        grid=(s // tm,),
        in_specs=[row(d), _full(mod.shape), _full(ng.shape), row(LANES), row(LANES), _full(w_in_p.shape),
                  _full((1, MLA_Q_RANK)), _full((1, MLA_KV_RANK)), _full(wq_p.shape), _full(wk.shape),
                  _full(wv_t.shape)],
        out_specs=[row(hq), row(hq), pl.BlockSpec((MLA_HEADS, 1, MLA_V_ROWS, tm), lambda i: (0, i, 0, 0))],
        compiler_params=_params("arbitrary"),
        name="mla_in",
    )(x, mod, ng, rope_a, rope_b, w_in_p, q_norm.reshape(1, -1), kv_norm.reshape(1, -1), wq_p, wk, wv_t)


def _interleave(producers, consumers, lead=2):
    order = list(producers[:lead])
    rest = list(producers[lead:])
    for f in consumers:
        order.append(f)
        if rest:
            order.append(rest.pop(0))
    return order + rest


def _mla_attn_kernel(q_ref, k_ref, v_ref, o_ref, *scratch, bq, bk, sub):
    i = pl.program_id(1)
    groups = bq // sub
    ratio = bq // bk
    assert bq == ratio * bk and ratio % 2 == 0
    m_refs, acc_refs = scratch[:groups], scratch[groups:2 * groups]
    s_refs = (scratch[2 * groups:3 * groups], scratch[3 * groups:4 * groups])
    qt_ref = scratch[4 * groups]
    qt_ref[...] = q_ref[...].astype(F32).T.astype(BF16)
    for c in range(groups):
        m_refs[c][...] = jnp.full((1, sub), MASK_VALUE, F32)
        acc_refs[c][...] = jnp.zeros((MLA_V_ROWS, sub), F32)

    def offset_of(c, diag):
        return None if diag is None else c * sub - diag * bk

    def live_groups(diag):
        return [c for c in range(groups) if diag is None or offset_of(c, diag) >= -(sub - 1)]

    def producers(slot, j, diag):
        start = pl.multiple_of(j * bk, bk)
        k = k_ref[pl.ds(start, bk), :]

        def make(c):
            def produce():
                s_refs[slot][c][...] = _dot(k, qt_ref[:, c * sub:(c + 1) * sub])
            return produce
        return [make(c) for c in live_groups(diag)]

    def consumers(slot, j, diag):
        v = v_ref[0, j]

        def make(c):
            offset = offset_of(c, diag)

            def consume():
                s = s_refs[slot][c][...]
                if offset is not None and offset < bk - 1:
                    key = lax.broadcasted_iota(jnp.int32, (bk, sub), 0)
                    qry = lax.broadcasted_iota(jnp.int32, (bk, sub), 1)
                    s = jnp.where(key - qry <= offset, s, MASK_VALUE)
                m_prev = m_refs[c][...]
                m_new = jnp.maximum(m_prev, jnp.max(s, axis=0, keepdims=True))
                alpha = jnp.exp2(m_prev - m_new)
                p = jnp.exp2(s - m_new)
                m_refs[c][...] = m_new
                acc_refs[c][...] = alpha * acc_refs[c][...] + _dot(v, p.astype(BF16))
            return consume
        return [make(c) for c in live_groups(diag)]

    def stage(slot, j, diag, next_j, next_diag):
        nxt = [] if next_j is None else producers(1 - slot, next_j, next_diag)
        for f in _interleave(nxt, consumers(slot, j, diag)):
            f()

    for f in producers(0, 0, None):
        f()

    def body(t, carry):
        for u in range(ratio):
            j = ratio * t + u
            stage(u % 2, j, None, j + 1, None)
        return carry

    first_diag = i * ratio
    lax.fori_loop(0, i, body, 0)
    for dj in range(ratio):
        last = dj == ratio - 1
        stage(dj % 2, first_diag + dj, dj, None if last else first_diag + dj + 1, None if last else dj + 1)
    for c in range(groups):
        acc = acc_refs[c][...]
        o_ref[c * sub:(c + 1) * sub, :] = (acc[:MLA_V] / acc[MLA_V:MLA_V + 1]).T.astype(o_ref.dtype)


def _mla_attn(q, k, v_t):
    s = q.shape[0]
    bq, bk, sub = MLA_Q_BLOCK, MLA_BLOCK, MLA_Q_GROUP
    groups = bq // sub
    return pl.pallas_call(
        functools.partial(_mla_attn_kernel, bq=bq, bk=bk, sub=sub),
        out_shape=jax.ShapeDtypeStruct((s, MLA_HEADS * MLA_V), BF16),
        grid=(MLA_HEADS, s // bq),
        in_specs=[pl.BlockSpec((bq, MLA_QK_PAD), lambda h, i: (i, h)),
                  pl.BlockSpec((s, MLA_QK_PAD), lambda h, i: (0, h)),
                  pl.BlockSpec((1, s // bk, MLA_V_ROWS, bk), lambda h, i: (h, 0, 0, 0))],
        out_specs=pl.BlockSpec((bq, MLA_V), lambda h, i: (i, h)),
        scratch_shapes=([pltpu.VMEM((1, sub), F32)] * groups + [pltpu.VMEM((MLA_V_ROWS, sub), F32)] * groups
                        + [pltpu.VMEM((bk, sub), F32)] * (2 * groups) + [pltpu.VMEM((MLA_QK_PAD, bq), BF16)]),
        compiler_params=_params("arbitrary", "arbitrary"),
        name="mla_attn",
    )(q, k, v_t)


def _mla_mixer(x, mod, ng, tables, w_in, q_norm, kv_norm, w_q_up, w_kv_up):
    q, k, v = _mla_in(x, mod, ng, tables[0], tables[1], w_in, q_norm, kv_norm, w_q_up, w_kv_up)
    return _mla_attn(q, k, v)


def _sb_in_kernel(x_ref, mod_ref, ng_ref, wqv_ref, wkt_ref, q_ref, kt_ref, v_ref, *, q_scale, bk):
    h = _modulated(x_ref[...], ng_ref[0:1, :], mod_ref[0:1, :], mod_ref[1:2, :])
    qv = _dot(h, wqv_ref[...])
    n = SB_HEADS * SB_HEAD_DIM
    q_ref[...] = (qv[:, :n] * q_scale).astype(BF16)
    v_ref[...] = qv[:, n:].astype(BF16)
    k_t = _dot_nt(wkt_ref[...], h).astype(BF16)
    for b in range(k_t.shape[1] // bk):
        kt_ref[b] = k_t[:, b * bk:(b + 1) * bk]


def _sb_in(x, mod, ng, w_in):
    s, d = x.shape
    tm, bk = ROW_TILE, SB_K_BLOCK
    n = SB_HEADS * SB_HEAD_DIM
    row = lambda w: pl.BlockSpec((tm, w), lambda i: (i, 0))
    w = w_in.astype(BF16)
    w_qv = jnp.concatenate([w[:, :n], w[:, 2 * n:]], axis=1)
    wk_t = w[:, n:2 * n].T
    return pl.pallas_call(
        functools.partial(_sb_in_kernel, q_scale=-LOG2E * SB_HEAD_DIM ** -0.5, bk=bk),
        out_shape=[jax.ShapeDtypeStruct((s, n), BF16), jax.ShapeDtypeStruct((s // bk, n, bk), BF16),
                   jax.ShapeDtypeStruct((s, n), BF16)],
        grid=(s // tm,),
        in_specs=[row(d), _full(mod.shape), _full(ng.shape), _full(w_qv.shape), _full(wk_t.shape)],
        out_specs=[row(n), pl.BlockSpec((tm // bk, n, bk), lambda i: (i, 0, 0)), row(n)],
        compiler_params=_params("arbitrary"),
        name="sb_in",
    )(x, mod, ng, w_qv, wk_t)


def _sb_attn_kernel(q_ref, kt_ref, v_ref, tri_ref, o_ref, acc_ref, *scratch, bq, bk, heads):
    i = pl.program_id(1)
    ratio = bq // bk
    assert bq == ratio * bk and heads % SB_GROUP_HEADS == 0
    gw = SB_GROUP_HEADS * SB_HEAD_DIM
    c_refs = scratch[:heads]
    w_refs = (scratch[heads:2 * heads], scratch[2 * heads:3 * heads])
    lane_q = lax.broadcasted_iota(jnp.int32, (bq, gw), 1)
    lane_v = lax.broadcasted_iota(jnp.int32, (bk, gw), 1)

    def group_of(hd):
        return slice((hd // SB_GROUP_HEADS) * gw, (hd // SB_GROUP_HEADS + 1) * gw)

    def head_lanes(lane, hd):
        lo = (hd % SB_GROUP_HEADS) * SB_HEAD_DIM
        return (lane >= lo) & (lane < lo + SB_HEAD_DIM)

    def only_head(x, lane, hd):
        return jnp.where(head_lanes(lane, hd), x, jnp.zeros_like(x))

    q_heads = [only_head(q_ref[:, group_of(hd)], lane_q, hd) for hd in range(heads)]
    acc_ref[...] = jnp.zeros(acc_ref.shape, F32)
    for hd in range(heads):
        c_refs[hd][...] = jnp.zeros((bq, LANES), F32)
    tri = tri_ref[...]

    def producers(slot, j):
        k_t = kt_ref[j]

        def make(hd):
            def produce():
                w_refs[slot][hd][...] = _dot(q_heads[hd], k_t[group_of(hd), :])
            return produce
        return [make(hd) for hd in range(heads)]

    def stage(slot, j, mask_offset, next_j):
        start = pl.multiple_of(j * bk, bk)
        v = v_ref[pl.ds(start, bk), :]
        if mask_offset is not None:
            row = lax.broadcasted_iota(jnp.int32, (bq, bk), 0)
            col = lax.broadcasted_iota(jnp.int32, (bq, bk), 1)
            strict = (col - row) < mask_offset
        weights = [None] * heads

        def make(hd):
            def consume():
                w = w_refs[slot][hd][...]
                neg_log_b = jnp.log2(1.0 + jnp.exp2(jnp.minimum(w, SB_EXP2_CLAMP)))
                log_1m = jnp.minimum(w - neg_log_b, 0.0)
                if mask_offset is not None:
                    log_1m = jnp.where(strict, log_1m, 0.0)
                suffix = _dot(log_1m.astype(BF16), tri)
                c = c_refs[hd][...]
                a = jnp.exp2(suffix - neg_log_b + _tile_lanes(c, bk))
                if mask_offset is not None:
                    a = jnp.where(strict, a, 0.0)
                weights[hd] = a.astype(BF16)
                c_refs[hd][...] = c + jnp.sum(log_1m, axis=1, keepdims=True)
            return consume

        for f in _interleave(producers(1 - slot, next_j), [make(hd) for hd in range(heads)]):
            f()
        for g in range(heads // SB_GROUP_HEADS):
            members = range(g * SB_GROUP_HEADS, (g + 1) * SB_GROUP_HEADS)
            v_heads = [only_head(v[:, group_of(hd)], lane_v, hd) for hd in members]
            acc_ref[:, g * gw:(g + 1) * gw] += _dot(jnp.concatenate([weights[hd] for hd in members], axis=1),
                                                    jnp.concatenate(v_heads, axis=0))

    last = ratio * i + ratio - 1
    for f in producers(0, last):
        f()
    for p in range(ratio):
        dj = ratio - 1 - p
        stage(p % 2, ratio * i + dj, -dj * bk, jnp.maximum(ratio * i + dj - 1, 0))

    def any_weight_left():
        worst = c_refs[0][...]
        for hd in range(1, heads):
            worst = jnp.maximum(worst, c_refs[hd][...])
        return (jnp.max(worst) > -SB_EXP2_UNDERFLOW).astype(jnp.int32)

    def more(carry):
        n, left = carry
        return jnp.logical_and(n < ratio * i, left > 0)

    def body(carry):
        n, _ = carry
        j = ratio * i - 1 - n
        nxt = jnp.maximum(j - 1, 0)
        for slot in range(2):
            @pl.when((n + ratio) % 2 == slot)
            def _():
                stage(slot, j, None, nxt)
        return n + 1, any_weight_left()

    lax.while_loop(more, body, (0, any_weight_left()))
    o_ref[...] = acc_ref[...].astype(o_ref.dtype)


def _sb_attn(q, k_t, v):
    s = q.shape[0]
    bq, bk, heads = SB_Q_BLOCK, SB_K_BLOCK, SB_HEADS_PER_STEP
    width = heads * SB_HEAD_DIM
    tri = jnp.tril(jnp.ones((bk, bk), F32), k=-1).astype(BF16)
    return pl.pallas_call(
        functools.partial(_sb_attn_kernel, bq=bq, bk=bk, heads=heads),
        out_shape=jax.ShapeDtypeStruct((s, SB_HEADS * SB_HEAD_DIM), BF16),
        grid=(SB_HEADS // heads, s // bq),
        in_specs=[pl.BlockSpec((bq, width), lambda p, i: (i, p)),
                  pl.BlockSpec((s // bk, width, bk), lambda p, i: (0, p, 0), pipeline_mode=pl.Buffered(1)),
                  pl.BlockSpec((s, width), lambda p, i: (0, p), pipeline_mode=pl.Buffered(1)), _full((bk, bk))],
        out_specs=pl.BlockSpec((bq, width), lambda p, i: (i, p)),
        scratch_shapes=([pltpu.VMEM((bq, width), F32)] + [pltpu.VMEM((bq, LANES), F32)] * heads
                        + [pltpu.VMEM((bq, bk), F32)] * (2 * heads)),
        compiler_params=_params("arbitrary", "arbitrary"),
        name="sb_attn",
    )(q, k_t, v, tri)


def _ret_in_kernel(x_ref, mod_ref, ng_ref, cos_ref, sin_ref, w_ref, q_ref, k_ref, v_ref, g_ref, *, k_scale):
    h = _modulated(x_ref[...], ng_ref[0:1, :], mod_ref[0:1, :], mod_ref[1:2, :])
    y = _dot(h, w_ref[...])
    hk, hv = RET_HEADS * RET_KEY_DIM, RET_HEADS * RET_VAL_DIM
    cos, sin = cos_ref[...], sin_ref[...]
    half = RET_KEY_DIM // 2

    def rope(t, scale):
        parts = []
        for hd in range(RET_HEADS):
            t1 = t[:, hd * RET_KEY_DIM:hd * RET_KEY_DIM + half]
            t2 = t[:, hd * RET_KEY_DIM + half:(hd + 1) * RET_KEY_DIM]
            parts.append(((t1 * cos - t2 * sin) * scale).astype(BF16))
            parts.append(((t2 * cos + t1 * sin) * scale).astype(BF16))
        return jnp.concatenate(parts, axis=1)

    q_ref[...] = rope(y[:, :hk], 1.0)
    k_ref[...] = rope(y[:, hk:2 * hk], k_scale)
    v_ref[...] = y[:, 2 * hk:2 * hk + hv].astype(BF16)
    g_ref[...] = y[:, 2 * hk + hv:]


def _ret_in(x, mod, ng, cos, sin, w_in):
    s, d = x.shape
    tm = ROW_TILE
    hk, hv = RET_HEADS * RET_KEY_DIM, RET_HEADS * RET_VAL_DIM
    row = lambda w: pl.BlockSpec((tm, w), lambda i: (i, 0))
    w = w_in.astype(BF16)
    return pl.pallas_call(
        functools.partial(_ret_in_kernel, k_scale=RET_KEY_DIM ** -0.5),
        out_shape=[jax.ShapeDtypeStruct((s, hk), BF16), jax.ShapeDtypeStruct((s, hk), BF16),
                   jax.ShapeDtypeStruct((s, hv), BF16), jax.ShapeDtypeStruct((s, hv), F32)],
        grid=(s // tm,),
        in_specs=[row(d), _full(mod.shape), _full(ng.shape), row(LANES), row(LANES), _full(w.shape)],
        out_specs=[row(hk), row(hk), row(hv), row(hv)],
        compiler_params=_params("arbitrary"),
        name="ret_in",
    )(x, mod, ng, cos, sin, w)


def _ret_core_kernel(lg_ref, q_ref, k_ref, v_ref, g_ref, gn_ref, o_ref, state_ref, decay_ref, xi_ref, zeta_ref, *,
                     chunk):
    @pl.when(pl.program_id(0) == 0)
    def _():
        state_ref[...] = jnp.zeros(state_ref.shape, F32)
        row = lax.broadcasted_iota(jnp.int32, (chunk, chunk), 0)
        col = lax.broadcasted_iota(jnp.int32, (chunk, chunk), 1)
        diff = (row - col).astype(F32)
        idx = lax.broadcasted_iota(jnp.int32, (chunk, LANES), 0).astype(F32)
        for hd in range(RET_HEADS):
            lg_row = lg_ref[hd][0:1, :]
            decay_ref[hd] = jnp.where(diff >= 0.0,
                                      jnp.exp(jnp.maximum(diff, 0.0) * _tile_lanes(lg_row, chunk)), 0.0)
            xi_ref[hd] = jnp.exp((idx + 1.0) * lg_row)
            zeta_ref[hd] = jnp.exp((chunk - 1.0 - idx) * lg_row)

    for hd in range(RET_HEADS):
        decay, xi, zeta = decay_ref[hd], xi_ref[hd], zeta_ref[hd]
        g_chunk = jnp.exp(chunk * lg_ref[hd])
        q = q_ref[:, hd * RET_KEY_DIM:(hd + 1) * RET_KEY_DIM]
        k = k_ref[:, hd * RET_KEY_DIM:(hd + 1) * RET_KEY_DIM]
        v = v_ref[:, hd * RET_VAL_DIM:(hd + 1) * RET_VAL_DIM]
        state = state_ref[hd]
        sc = _dot_nt(q, k) * decay
        inner = _dot(sc.astype(BF16), v)
        cross = _dot(q, state.astype(BF16)) * _tile_lanes(xi, RET_VAL_DIM)
        kz = (k.astype(F32) * _tile_lanes(zeta, RET_KEY_DIM)).astype(BF16)
        upd = lax.dot_general(kz, v, (((0,), (0,)), ((), ())), preferred_element_type=F32)
        state_ref[hd] = state * _tile_lanes(g_chunk[0:1, :], RET_VAL_DIM) + upd
        o = inner + cross
        o = o * lax.rsqrt(jnp.mean(o * o, axis=-1, keepdims=True) + EPS)
        o = o * gn_ref[:, hd * RET_VAL_DIM:(hd + 1) * RET_VAL_DIM]
        g = g_ref[:, hd * RET_VAL_DIM:(hd + 1) * RET_VAL_DIM]
        o_ref[:, hd * RET_VAL_DIM:(hd + 1) * RET_VAL_DIM] = (g * (1.0 / (1.0 + jnp.exp(-g))) * o).astype(o_ref.dtype)


def _ret_core(q, k, v, g, gn_g):
    s = q.shape[0]
    chunk = RET_CHUNK
    hk, hv = RET_HEADS * RET_KEY_DIM, RET_HEADS * RET_VAL_DIM
    log_gamma = jnp.log(1.0 - 2.0 ** (-5.0 - jnp.arange(RET_HEADS, dtype=F32)))
    lg = jnp.broadcast_to(log_gamma[:, None, None], (RET_HEADS, SUBLANES, LANES))
    row = lambda w: pl.BlockSpec((chunk, w), lambda i: (i, 0))
    return pl.pallas_call(
        functools.partial(_ret_core_kernel, chunk=chunk),
        out_shape=jax.ShapeDtypeStruct((s, hv), BF16),
        grid=(s // chunk,),
        in_specs=[_full(lg.shape), row(hk), row(hk), row(hv), row(hv), _full((1, hv))],
        out_specs=row(hv),
        scratch_shapes=[pltpu.VMEM((RET_HEADS, RET_KEY_DIM, RET_VAL_DIM), F32),
                        pltpu.VMEM((RET_HEADS, chunk, chunk), F32), pltpu.VMEM((RET_HEADS, chunk, LANES), F32),
                        pltpu.VMEM((RET_HEADS, chunk, LANES), F32)],
        compiler_params=_params("arbitrary"),
        name="ret_core",
    )(lg, q, k, v, g, gn_g.reshape(1, hv))


def _post_kernel(o_ref, x_ref, mod_ref, ng_ref, wo_ref, w1_ref, w2_ref, out_ref, y_ref):
    x = x_ref[...]
    y = _dot(o_ref[...], wo_ref[...])
    x = x + mod_ref[2:3, :] * _rms(y, ng_ref[1:2, :])
    h = _modulated(x, ng_ref[2:3, :], mod_ref[3:4, :], mod_ref[4:5, :])
    d_ff = w1_ref.shape[1]
    for c in range(d_ff // FF_CHUNK):
        a = jnp.maximum(_dot(h, w1_ref[:, c * FF_CHUNK:(c + 1) * FF_CHUNK]), 0.0)
        part = _dot((a * a).astype(BF16), w2_ref[c * FF_CHUNK:(c + 1) * FF_CHUNK, :])
        if c == 0:
            y_ref[...] = part
        else:
            y_ref[...] += part
    out_ref[...] = x + mod_ref[5:6, :] * _rms(y_ref[...], ng_ref[3:4, :])


def _post(o, x, mod, ng, w_out, w1_all, w2_all, layer):
    s, d = x.shape
    tm = POST_ROW_TILE
    ko = o.shape[1]
    row = lambda w: pl.BlockSpec((tm, w), lambda i: (i, 0))
    resident = lambda a: pl.BlockSpec(a.shape, lambda i: (0, 0), pipeline_mode=pl.Buffered(1))
    of_layer = lambda a: pl.BlockSpec((None,) + a.shape[1:], lambda i: (layer, 0, 0), pipeline_mode=pl.Buffered(1))
    wo = w_out.astype(BF16)
    return pl.pallas_call(
        _post_kernel,
        out_shape=jax.ShapeDtypeStruct((s, d), F32),
        grid=(s // tm,),
        in_specs=[row(ko), row(d), _full(mod.shape), _full(ng.shape), resident(wo), of_layer(w1_all),
                  of_layer(w2_all)],
        out_specs=row(d),
        scratch_shapes=[pltpu.VMEM((tm, d), F32)],
        compiler_params=_params("arbitrary"),
        name="post",
    )(o, x, mod, ng, wo, w1_all, w2_all)


def kernel(x, c, positions, ada_w, ada_b, norm_g, ffn_w1, ffn_w2, mla_w_in, mla_q_norm, mla_kv_norm, mla_w_q_up,
           mla_w_kv_up, mla_w_out, sb_w_in, sb_w_out, ret_w_in, ret_gn_g, ret_w_out):
    batch, seq, d = x.shape
    assert batch == 1 and seq % MLA_Q_BLOCK == 0 and seq % ROW_TILE == 0 and seq % POST_ROW_TILE == 0
    depth = ada_w.shape[0]
    mod = _adaln(c, ada_w, ada_b)
    tables = _rope_tables(positions)
    w1_all, w2_all = ffn_w1.astype(BF16), ffn_w2.astype(BF16)
    xs = x.reshape(seq, d)
    for i in range(depth):
        kind, j = i % 3, i // 3
        if kind == 0:
            o = _mla_mixer(xs, mod[i], norm_g[i], tables, mla_w_in[j], mla_q_norm[j], mla_kv_norm[j],
                           mla_w_q_up[j], mla_w_kv_up[j])
            w_out = mla_w_out[j]
        elif kind == 1:
            q, k, v = _sb_in(xs, mod[i], norm_g[i], sb_w_in[j])
            o, w_out = _sb_attn(q, k, v), sb_w_out[j]
        else:
            q, k, v, g = _ret_in(xs, mod[i], norm_g[i], tables[2], tables[3], ret_w_in[j])
            o, w_out = _ret_core(q, k, v, g, ret_gn_g[j]), ret_w_out[j]
        xs = _post(o, xs, mod[i], norm_g[i], w_out, w1_all, w2_all, i)
    return xs.reshape(batch, seq, d)
```

```python
import functools
import math

import jax
import jax.numpy as jnp
from jax import lax
from jax.experimental import pallas as pl
from jax.experimental.pallas import tpu as pltpu

EPS = 1e-6
ROPE_BASE = 10000.0
MASK_VALUE = -1e30
LOG2E = math.log2(math.e)

LANES = 128
SUBLANES = 8
VMEM_LIMIT_BYTES = 60000 * 1024

MLA_HEADS = 8
MLA_Q_RANK = 256
MLA_KV_RANK = 128
MLA_NOPE = 128
MLA_ROPE = 64
MLA_V = 128
MLA_QK_PAD = 256

SB_HEADS = 16
SB_HEAD_DIM = 64

RET_HEADS = 4
RET_KEY_DIM = 256
RET_VAL_DIM = 512

ROW_TILE = 512
POST_ROW_TILE = 1024
FF_CHUNK = 1024
MLA_BLOCK = 512
MLA_Q_BLOCK = 2048
MLA_Q_GROUP = 256
SB_Q_BLOCK = 256
SB_K_BLOCK = 256
SB_HEADS_PER_STEP = 8
SB_GROUP_HEADS = 4
SB_EXP2_CLAMP = 126.0
SB_EXP2_UNDERFLOW = 160.0
RET_CHUNK = 256

F32 = jnp.float32
BF16 = jnp.bfloat16


def _params(*semantics):
    return pltpu.CompilerParams(dimension_semantics=semantics, vmem_limit_bytes=VMEM_LIMIT_BYTES)


def _full(shape):
    nd = len(shape)
    return pl.BlockSpec(shape, lambda *_: (0,) * nd)


def _rms(x, g):
    return x * lax.rsqrt(jnp.mean(x * x, axis=-1, keepdims=True) + EPS) * g


def _dot(a, b):
    return jnp.dot(a, b, preferred_element_type=F32)


def _dot_nt(a, b):
    return lax.dot_general(a, b, (((1,), (1,)), ((), ())), preferred_element_type=F32)


def _tile_lanes(v, width):
    reps = width // LANES
    return v if reps == 1 else jnp.concatenate([v] * reps, axis=1)


def _adaln_kernel(c_ref, w_ref, b_ref, o_ref):
    c = c_ref[...]
    cond = c * (1.0 / (1.0 + jnp.exp(-c)))
    o_ref[0] = jnp.sum(w_ref[0] * cond, axis=0, keepdims=True) + b_ref[0]


def _adaln(c, ada_w, ada_b):
    depth, d, n = ada_w.shape
    tn = 768
    out = pl.pallas_call(
        _adaln_kernel,
        out_shape=jax.ShapeDtypeStruct((depth, 1, n), F32),
        grid=(depth, n // tn),
        in_specs=[
            _full((d, 1)),
            pl.BlockSpec((1, d, tn), lambda i, j: (i, 0, j)),
            pl.BlockSpec((1, 1, tn), lambda i, j: (i, 0, j)),
        ],
        out_specs=pl.BlockSpec((1, 1, tn), lambda i, j: (i, 0, j)),
        compiler_params=_params("arbitrary", "arbitrary"),
        name="adaln",
    )(c.reshape(d, 1), ada_w, ada_b.reshape(depth, 1, n))
    return out.reshape(depth, 6, d)


def _rope_kernel(pos_ref, f64_ref, f256_ref, a_mask_ref, b_sign_ref, a_ref, b_ref, cos_ref, sin_ref):
    pos = pos_ref[...]
    ang = pos * f64_ref[...]
    a_ref[...] = jnp.cos(ang) * a_mask_ref[...]
    b_ref[...] = jnp.sin(ang) * b_sign_ref[...]
    ang = pos * f256_ref[...]
    cos_ref[...] = jnp.cos(ang)
    sin_ref[...] = jnp.sin(ang)


def _rope_tables(positions):
    s = positions.shape[-1]
    tm = 1024
    pos = positions.astype(F32).reshape(s, 1)
    f64 = ROPE_BASE ** (-jnp.arange(0, MLA_ROPE, 2, dtype=F32) / MLA_ROPE)
    f256 = ROPE_BASE ** (-jnp.arange(0, RET_KEY_DIM, 2, dtype=F32) / RET_KEY_DIM)
    quarter = MLA_ROPE // 2
    ones, zeros = jnp.ones((quarter,), F32), jnp.zeros((quarter,), F32)
    a_mask = jnp.concatenate([ones, ones, zeros, zeros]).reshape(1, LANES)
    b_sign = jnp.concatenate([-ones, ones, zeros, zeros]).reshape(1, LANES)
    row = pl.BlockSpec((tm, LANES), lambda i: (i, 0))
    return pl.pallas_call(
        _rope_kernel,
        out_shape=[jax.ShapeDtypeStruct((s, LANES), F32)] * 4,
        grid=(s // tm,),
        in_specs=[pl.BlockSpec((tm, 1), lambda i: (i, 0))] + [_full((1, LANES))] * 4,
        out_specs=[row] * 4,
        compiler_params=_params("arbitrary"),
        name="rope_tables",
    )(pos, jnp.tile(f64, 4).reshape(1, LANES), f256.reshape(1, LANES), a_mask, b_sign)


def _modulated(x, g, shift, scale):
    return (_rms(x, g) * (1.0 + scale) + shift).astype(BF16)


def _mla_in_kernel(x_ref, mod_ref, ng_ref, a_ref, b_ref, w_in_ref, qn_ref, kvn_ref, wq_ref, wk_ref, wv_ref,
                   q_ref, k_ref, v_ref, *, q_scale):
    h = _modulated(x_ref[...], ng_ref[0:1, :], mod_ref[0:1, :], mod_ref[1:2, :])
    lat = _dot(h, w_in_ref[...])
    rope_a, rope_b = a_ref[...], b_ref[...]

    def rope(v):
        return v * rope_a + pltpu.roll(v, 2 * (MLA_ROPE // 2), axis=1) * rope_b

    q_lat = lat[:, :MLA_Q_RANK]
    kv_lat = lat[:, MLA_Q_RANK:MLA_Q_RANK + MLA_KV_RANK]
    k_pe = rope(lat[:, MLA_Q_RANK + MLA_KV_RANK:]).astype(BF16)
    q = _dot(_rms(q_lat, qn_ref[...]).astype(BF16), wq_ref[...])
    kvn = _rms(kv_lat, kvn_ref[...]).astype(BF16)
    k_nope = _dot(kvn, wk_ref[...]).astype(BF16)
    v_t = _dot_nt(wv_ref[...], kvn).astype(BF16)
    v_ref[...] = v_t.reshape(MLA_HEADS, 1, MLA_V, v_t.shape[1])
    q_parts, k_parts = [], []
    for hd in range(MLA_HEADS):
        lo = hd * MLA_QK_PAD
        q_parts.append((q[:, lo:lo + MLA_NOPE] * q_scale).astype(BF16))
        q_parts.append((rope(q[:, lo + MLA_NOPE:lo + MLA_QK_PAD]) * q_scale).astype(BF16))
        k_parts.append(k_nope[:, hd * MLA_NOPE:(hd + 1) * MLA_NOPE])
        k_parts.append(k_pe)
    q_ref[...] = jnp.concatenate(q_parts, axis=1)
    k_ref[...] = jnp.concatenate(k_parts, axis=1)


def _mla_weights(w_in, w_q_up, w_kv_up):
    half = MLA_ROPE // 2
    pe = w_in[:, MLA_Q_RANK + MLA_KV_RANK:]
    pe1, pe2 = pe[:, :half], pe[:, half:]
    w_in_p = jnp.concatenate([w_in[:, :MLA_Q_RANK + MLA_KV_RANK], pe1, pe2, pe2, pe1], axis=1)
    wq = w_q_up.reshape(MLA_Q_RANK, MLA_HEADS, MLA_NOPE + MLA_ROPE)
    q1, q2 = wq[..., MLA_NOPE:MLA_NOPE + half], wq[..., MLA_NOPE + half:]
    wq_p = jnp.concatenate([wq[..., :MLA_NOPE], q1, q2, q2, q1], axis=-1).reshape(MLA_Q_RANK, MLA_HEADS * MLA_QK_PAD)
    wkv = w_kv_up.reshape(MLA_KV_RANK, MLA_HEADS, MLA_NOPE + MLA_V)
    wk = wkv[..., :MLA_NOPE].reshape(MLA_KV_RANK, MLA_HEADS * MLA_NOPE)
    wv_t = wkv[..., MLA_NOPE:].reshape(MLA_KV_RANK, MLA_HEADS * MLA_V).T
    return w_in_p.astype(BF16), wq_p.astype(BF16), wk.astype(BF16), wv_t.astype(BF16)


def _mla_in(x, mod, ng, rope_a, rope_b, w_in, q_norm, kv_norm, w_q_up, w_kv_up):
    s, d = x.shape
    tm = MLA_BLOCK
    w_in_p, wq_p, wk, wv_t = _mla_weights(w_in, w_q_up, w_kv_up)
    q_scale = (MLA_NOPE + MLA_ROPE) ** -0.5 * LOG2E
    row = lambda n: pl.BlockSpec((tm, n), lambda i: (i, 0))
    hq = MLA_HEADS * MLA_QK_PAD
    return pl.pallas_call(
        functools.partial(_mla_in_kernel, q_scale=q_scale),
        out_shape=[jax.ShapeDtypeStruct((s, hq), BF16), jax.ShapeDtypeStruct((s, hq), BF16),
                   jax.ShapeDtypeStruct((MLA_HEADS, s // tm, MLA_V, tm), BF16)],
        grid=(s // tm,),
        in_specs=[row(d), _full(mod.shape), _full(ng.shape), row(LANES), row(LANES), _full(w_in_p.shape),
                  _full((1, MLA_Q_RANK)), _full((1, MLA_KV_RANK)), _full(wq_p.shape), _full(wk.shape),
                  _full(wv_t.shape)],
        out_specs=[row(hq), row(hq), pl.BlockSpec((MLA_HEADS, 1, MLA_V, tm), lambda i: (0, i, 0, 0))],
        compiler_params=_params("arbitrary"),
        name="mla_in",
    )(x, mod, ng, rope_a, rope_b, w_in_p, q_norm.reshape(1, -1), kv_norm.reshape(1, -1), wq_p, wk, wv_t)


def _interleave(producers, consumers, lead=2):
    order = list(producers[:lead])
    rest = list(producers[lead:])
    for f in consumers:
        order.append(f)
        if rest:
            order.append(rest.pop(0))
    return order + rest


def _mla_attn_kernel(q_ref, k_ref, v_ref, o_ref, *scratch, bq, bk, sub):
    i = pl.program_id(1)
    groups = bq // sub
    ratio = bq // bk
    assert bq == ratio * bk and ratio % 2 == 0
    m_refs, l_refs = scratch[:groups], scratch[groups:2 * groups]
    acc_refs = scratch[2 * groups:3 * groups]
    s_refs = (scratch[3 * groups:4 * groups], scratch[4 * groups:5 * groups])
    qt_ref = scratch[5 * groups]
    qt_ref[...] = q_ref[...].astype(F32).T.astype(BF16)
    for c in range(groups):
        m_refs[c][...] = jnp.full((1, sub), MASK_VALUE, F32)
        l_refs[c][...] = jnp.zeros((1, sub), F32)
        acc_refs[c][...] = jnp.zeros((MLA_V, sub), F32)

    def offset_of(c, diag):
        return None if diag is None else c * sub - diag * bk

    def live_groups(diag):
        return [c for c in range(groups) if diag is None or offset_of(c, diag) >= -(sub - 1)]

    def producers(slot, j, diag):
        start = pl.multiple_of(j * bk, bk)
        k = k_ref[pl.ds(start, bk), :]

        def make(c):
            def produce():
                s_refs[slot][c][...] = _dot(k, qt_ref[:, c * sub:(c + 1) * sub])
            return produce
        return [make(c) for c in live_groups(diag)]

    def consumers(slot, j, diag):
        v = v_ref[0, j]

        def make(c):
            offset = offset_of(c, diag)

            def consume():
                s = s_refs[slot][c][...]
                if offset is not None and offset < bk - 1:
                    key = lax.broadcasted_iota(jnp.int32, (bk, sub), 0)
                    qry = lax.broadcasted_iota(jnp.int32, (bk, sub), 1)
                    s = jnp.where(key - qry <= offset, s, MASK_VALUE)
                m_prev = m_refs[c][...]
                m_new = jnp.maximum(m_prev, jnp.max(s, axis=0, keepdims=True))
                alpha = jnp.exp2(m_prev - m_new)
                p = jnp.exp2(s - m_new)
                l_refs[c][...] = alpha * l_refs[c][...] + jnp.sum(p, axis=0, keepdims=True)
                m_refs[c][...] = m_new
                acc_refs[c][...] = alpha * acc_refs[c][...] + _dot(v, p.astype(BF16))
            return consume
        return [make(c) for c in live_groups(diag)]

    def stage(slot, j, diag, next_j, next_diag):
        nxt = [] if next_j is None else producers(1 - slot, next_j, next_diag)
        for f in _interleave(nxt, consumers(slot, j, diag)):
            f()

    for f in producers(0, 0, None):
        f()

    def body(t, carry):
        for u in range(ratio):
            j = ratio * t + u
            stage(u % 2, j, None, j + 1, None)
        return carry

    first_diag = i * ratio
    lax.fori_loop(0, i, body, 0)
    for dj in range(ratio):
        last = dj == ratio - 1
        stage(dj % 2, first_diag + dj, dj, None if last else first_diag + dj + 1, None if last else dj + 1)
    for c in range(groups):
        o_ref[c * sub:(c + 1) * sub, :] = (acc_refs[c][...] / l_refs[c][...]).T.astype(o_ref.dtype)


def _mla_attn(q, k, v_t):
    s = q.shape[0]
    bq, bk, sub = MLA_Q_BLOCK, MLA_BLOCK, MLA_Q_GROUP
    groups = bq // sub
    return pl.pallas_call(
        functools.partial(_mla_attn_kernel, bq=bq, bk=bk, sub=sub),
        out_shape=jax.ShapeDtypeStruct((s, MLA_HEADS * MLA_V), BF16),
        grid=(MLA_HEADS, s // bq),
        in_specs=[pl.BlockSpec((bq, MLA_QK_PAD), lambda h, i: (i, h)),
                  pl.BlockSpec((s, MLA_QK_PAD), lambda h, i: (0, h)),
                  pl.BlockSpec((1, s // bk, MLA_V, bk), lambda h, i: (h, 0, 0, 0))],
        out_specs=pl.BlockSpec((bq, MLA_V), lambda h, i: (i, h)),
        scratch_shapes=([pltpu.VMEM((1, sub), F32)] * (2 * groups) + [pltpu.VMEM((MLA_V, sub), F32)] * groups
                        + [pltpu.VMEM((bk, sub), F32)] * (2 * groups) + [pltpu.VMEM((MLA_QK_PAD, bq), BF16)]),
        compiler_params=_params("arbitrary", "arbitrary"),
        name="mla_attn",
    )(q, k, v_t)


def _mla_mixer(x, mod, ng, tables, w_in, q_norm, kv_norm, w_q_up, w_kv_up):
    q, k, v = _mla_in(x, mod, ng, tables[0], tables[1], w_in, q_norm, kv_norm, w_q_up, w_kv_up)
    return _mla_attn(q, k, v)


def _sb_in_kernel(x_ref, mod_ref, ng_ref, wqv_ref, wkt_ref, q_ref, kt_ref, v_ref, *, q_scale, bk):
    h = _modulated(x_ref[...], ng_ref[0:1, :], mod_ref[0:1, :], mod_ref[1:2, :])
    qv = _dot(h, wqv_ref[...])
    n = SB_HEADS * SB_HEAD_DIM
    q_ref[...] = (qv[:, :n] * q_scale).astype(BF16)
    v_ref[...] = qv[:, n:].astype(BF16)
    k_t = _dot_nt(wkt_ref[...], h).astype(BF16)
    for b in range(k_t.shape[1] // bk):
        kt_ref[b] = k_t[:, b * bk:(b + 1) * bk]


def _sb_in(x, mod, ng, w_in):
    s, d = x.shape
    tm, bk = ROW_TILE, SB_K_BLOCK
    n = SB_HEADS * SB_HEAD_DIM
    row = lambda w: pl.BlockSpec((tm, w), lambda i: (i, 0))
    w = w_in.astype(BF16)
    w_qv = jnp.concatenate([w[:, :n], w[:, 2 * n:]], axis=1)
    wk_t = w[:, n:2 * n].T
    return pl.pallas_call(
        functools.partial(_sb_in_kernel, q_scale=-LOG2E * SB_HEAD_DIM ** -0.5, bk=bk),
        out_shape=[jax.ShapeDtypeStruct((s, n), BF16), jax.ShapeDtypeStruct((s // bk, n, bk), BF16),
                   jax.ShapeDtypeStruct((s, n), BF16)],
        grid=(s // tm,),
        in_specs=[row(d), _full(mod.shape), _full(ng.shape), _full(w_qv.shape), _full(wk_t.shape)],
        out_specs=[row(n), pl.BlockSpec((tm // bk, n, bk), lambda i: (i, 0, 0)), row(n)],
        compiler_params=_params("arbitrary"),
        name="sb_in",
    )(x, mod, ng, w_qv, wk_t)


def _sb_attn_kernel(q_ref, kt_ref, v_ref, tri_ref, o_ref, acc_ref, *scratch, bq, bk, heads):
    i = pl.program_id(1)
    ratio = bq // bk
    assert bq == ratio * bk and heads % SB_GROUP_HEADS == 0
    gw = SB_GROUP_HEADS * SB_HEAD_DIM
    c_refs = scratch[:heads]
    w_refs = (scratch[heads:2 * heads], scratch[2 * heads:3 * heads])
    lane_q = lax.broadcasted_iota(jnp.int32, (bq, gw), 1)
    lane_v = lax.broadcasted_iota(jnp.int32, (bk, gw), 1)

    def group_of(hd):
        return slice((hd // SB_GROUP_HEADS) * gw, (hd // SB_GROUP_HEADS + 1) * gw)

    def head_lanes(lane, hd):
        lo = (hd % SB_GROUP_HEADS) * SB_HEAD_DIM
        return (lane >= lo) & (lane < lo + SB_HEAD_DIM)

    def only_head(x, lane, hd):
        return jnp.where(head_lanes(lane, hd), x, jnp.zeros_like(x))

    q_heads = [only_head(q_ref[:, group_of(hd)], lane_q, hd) for hd in range(heads)]
    acc_ref[...] = jnp.zeros(acc_ref.shape, F32)
    for hd in range(heads):
        c_refs[hd][...] = jnp.zeros((bq, LANES), F32)
    tri = tri_ref[...]

    def producers(slot, j):
        k_t = kt_ref[j]

        def make(hd):
            def produce():
                w_refs[slot][hd][...] = _dot(q_heads[hd], k_t[group_of(hd), :])
            return produce
        return [make(hd) for hd in range(heads)]

    def stage(slot, j, mask_offset, next_j):
        start = pl.multiple_of(j * bk, bk)
        v = v_ref[pl.ds(start, bk), :]
        if mask_offset is not None:
            row = lax.broadcasted_iota(jnp.int32, (bq, bk), 0)
            col = lax.broadcasted_iota(jnp.int32, (bq, bk), 1)
            strict = (col - row) < mask_offset
        weights = [None] * heads

        def make(hd):
            def consume():
                w = w_refs[slot][hd][...]
                neg_log_b = jnp.log2(1.0 + jnp.exp2(jnp.minimum(w, SB_EXP2_CLAMP)))
                log_1m = jnp.minimum(w - neg_log_b, 0.0)
                if mask_offset is not None:
                    log_1m = jnp.where(strict, log_1m, 0.0)
                suffix = _dot(log_1m.astype(BF16), tri)
                c = c_refs[hd][...]
                a = jnp.exp2(suffix - neg_log_b + _tile_lanes(c, bk))
                if mask_offset is not None:
                    a = jnp.where(strict, a, 0.0)
                weights[hd] = a.astype(BF16)
                c_refs[hd][...] = c + jnp.sum(log_1m, axis=1, keepdims=True)
            return consume

        for f in _interleave(producers(1 - slot, next_j), [make(hd) for hd in range(heads)]):
            f()
        for g in range(heads // SB_GROUP_HEADS):
            members = range(g * SB_GROUP_HEADS, (g + 1) * SB_GROUP_HEADS)
            v_heads = [only_head(v[:, group_of(hd)], lane_v, hd) for hd in members]
            acc_ref[:, g * gw:(g + 1) * gw] += _dot(jnp.concatenate([weights[hd] for hd in members], axis=1),
                                                    jnp.concatenate(v_heads, axis=0))

    last = ratio * i + ratio - 1
    for f in producers(0, last):
        f()
    for p in range(ratio):
        dj = ratio - 1 - p
        stage(p % 2, ratio * i + dj, -dj * bk, jnp.maximum(ratio * i + dj - 1, 0))

    def any_weight_left():
        worst = c_refs[0][...]
        for hd in range(1, heads):
            worst = jnp.maximum(worst, c_refs[hd][...])
        return (jnp.max(worst) > -SB_EXP2_UNDERFLOW).astype(jnp.int32)

    def more(carry):
        n, left = carry
        return jnp.logical_and(n < ratio * i, left > 0)

    def body(carry):
        n, _ = carry
        j = ratio * i - 1 - n
        nxt = jnp.maximum(j - 1, 0)
        for slot in range(2):
            @pl.when((n + ratio) % 2 == slot)
            def _():
                stage(slot, j, None, nxt)
        return n + 1, any_weight_left()

    lax.while_loop(more, body, (0, any_weight_left()))
    o_ref[...] = acc_ref[...].astype(o_ref.dtype)


def _sb_attn(q, k_t, v):
    s = q.shape[0]
    bq, bk, heads = SB_Q_BLOCK, SB_K_BLOCK, SB_HEADS_PER_STEP
    width = heads * SB_HEAD_DIM
    tri = jnp.tril(jnp.ones((bk, bk), F32), k=-1).astype(BF16)
    return pl.pallas_call(
        functools.partial(_sb_attn_kernel, bq=bq, bk=bk, heads=heads),
        out_shape=jax.ShapeDtypeStruct((s, SB_HEADS * SB_HEAD_DIM), BF16),
        grid=(SB_HEADS // heads, s // bq),
        in_specs=[pl.BlockSpec((bq, width), lambda p, i: (i, p)),
                  pl.BlockSpec((s // bk, width, bk), lambda p, i: (0, p, 0), pipeline_mode=pl.Buffered(1)),
                  pl.BlockSpec((s, width), lambda p, i: (0, p), pipeline_mode=pl.Buffered(1)), _full((bk, bk))],
        out_specs=pl.BlockSpec((bq, width), lambda p, i: (i, p)),
        scratch_shapes=([pltpu.VMEM((bq, width), F32)] + [pltpu.VMEM((bq, LANES), F32)] * heads
                        + [pltpu.VMEM((bq, bk), F32)] * (2 * heads)),
        compiler_params=_params("arbitrary", "arbitrary"),
        name="sb_attn",
    )(q, k_t, v, tri)


def _ret_in_kernel(x_ref, mod_ref, ng_ref, cos_ref, sin_ref, w_ref, q_ref, k_ref, v_ref, g_ref, *, k_scale):
    h = _modulated(x_ref[...], ng_ref[0:1, :], mod_ref[0:1, :], mod_ref[1:2, :])
    y = _dot(h, w_ref[...])
    hk, hv = RET_HEADS * RET_KEY_DIM, RET_HEADS * RET_VAL_DIM
    cos, sin = cos_ref[...], sin_ref[...]
    half = RET_KEY_DIM // 2

    def rope(t, scale):
        parts = []
        for hd in range(RET_HEADS):
            t1 = t[:, hd * RET_KEY_DIM:hd * RET_KEY_DIM + half]
            t2 = t[:, hd * RET_KEY_DIM + half:(hd + 1) * RET_KEY_DIM]
            parts.append(((t1 * cos - t2 * sin) * scale).astype(BF16))
            parts.append(((t2 * cos + t1 * sin) * scale).astype(BF16))
        return jnp.concatenate(parts, axis=1)

    q_ref[...] = rope(y[:, :hk], 1.0)
    k_ref[...] = rope(y[:, hk:2 * hk], k_scale)
    v_ref[...] = y[:, 2 * hk:2 * hk + hv].astype(BF16)
    g_ref[...] = y[:, 2 * hk + hv:]


def _ret_in(x, mod, ng, cos, sin, w_in):
    s, d = x.shape
    tm = ROW_TILE
    hk, hv = RET_HEADS * RET_KEY_DIM, RET_HEADS * RET_VAL_DIM
    row = lambda w: pl.BlockSpec((tm, w), lambda i: (i, 0))
    w = w_in.astype(BF16)
    return pl.pallas_call(
        functools.partial(_ret_in_kernel, k_scale=RET_KEY_DIM ** -0.5),
        out_shape=[jax.ShapeDtypeStruct((s, hk), BF16), jax.ShapeDtypeStruct((s, hk), BF16),
                   jax.ShapeDtypeStruct((s, hv), BF16), jax.ShapeDtypeStruct((s, hv), F32)],
        grid=(s // tm,),
        in_specs=[row(d), _full(mod.shape), _full(ng.shape), row(LANES), row(LANES), _full(w.shape)],
        out_specs=[row(hk), row(hk), row(hv), row(hv)],
        compiler_params=_params("arbitrary"),
        name="ret_in",
    )(x, mod, ng, cos, sin, w)


def _ret_core_kernel(lg_ref, q_ref, k_ref, v_ref, g_ref, gn_ref, o_ref, state_ref, decay_ref, xi_ref, zeta_ref, *,
                     chunk):
    @pl.when(pl.program_id(0) == 0)
    def _():
        state_ref[...] = jnp.zeros(state_ref.shape, F32)
        row = lax.broadcasted_iota(jnp.int32, (chunk, chunk), 0)
        col = lax.broadcasted_iota(jnp.int32, (chunk, chunk), 1)
        diff = (row - col).astype(F32)
        idx = lax.broadcasted_iota(jnp.int32, (chunk, LANES), 0).astype(F32)
        for hd in range(RET_HEADS):
            lg_row = lg_ref[hd][0:1, :]
            decay_ref[hd] = jnp.where(diff >= 0.0,
                                      jnp.exp(jnp.maximum(diff, 0.0) * _tile_lanes(lg_row, chunk)), 0.0)
            xi_ref[hd] = jnp.exp((idx + 1.0) * lg_row)
            zeta_ref[hd] = jnp.exp((chunk - 1.0 - idx) * lg_row)

    for hd in range(RET_HEADS):
        decay, xi, zeta = decay_ref[hd], xi_ref[hd], zeta_ref[hd]
        g_chunk = jnp.exp(chunk * lg_ref[hd])
        q = q_ref[:, hd * RET_KEY_DIM:(hd + 1) * RET_KEY_DIM]
        k = k_ref[:, hd * RET_KEY_DIM:(hd + 1) * RET_KEY_DIM]
        v = v_ref[:, hd * RET_VAL_DIM:(hd + 1) * RET_VAL_DIM]
        state = state_ref[hd]
        sc = _dot_nt(q, k) * decay
        inner = _dot(sc.astype(BF16), v)
        cross = _dot(q, state.astype(BF16)) * _tile_lanes(xi, RET_VAL_DIM)
        kz = (k.astype(F32) * _tile_lanes(zeta, RET_KEY_DIM)).astype(BF16)
        upd = lax.dot_general(kz, v, (((0,), (0,)), ((), ())), preferred_element_type=F32)
        state_ref[hd] = state * _tile_lanes(g_chunk[0:1, :], RET_VAL_DIM) + upd
        o = inner + cross
        o = o * lax.rsqrt(jnp.mean(o * o, axis=-1, keepdims=True) + EPS)
        o = o * gn_ref[:, hd * RET_VAL_DIM:(hd + 1) * RET_VAL_DIM]
        g = g_ref[:, hd * RET_VAL_DIM:(hd + 1) * RET_VAL_DIM]
        o_ref[:, hd * RET_VAL_DIM:(hd + 1) * RET_VAL_DIM] = (g * (1.0 / (1.0 + jnp.exp(-g))) * o).astype(o_ref.dtype)


def _ret_core(q, k, v, g, gn_g):
    s = q.shape[0]
    chunk = RET_CHUNK
    hk, hv = RET_HEADS * RET_KEY_DIM, RET_HEADS * RET_VAL_DIM
    log_gamma = jnp.log(1.0 - 2.0 ** (-5.0 - jnp.arange(RET_HEADS, dtype=F32)))
    lg = jnp.broadcast_to(log_gamma[:, None, None], (RET_HEADS, SUBLANES, LANES))
    row = lambda w: pl.BlockSpec((chunk, w), lambda i: (i, 0))
    return pl.pallas_call(
        functools.partial(_ret_core_kernel, chunk=chunk),
        out_shape=jax.ShapeDtypeStruct((s, hv), BF16),
        grid=(s // chunk,),
        in_specs=[_full(lg.shape), row(hk), row(hk), row(hv), row(hv), _full((1, hv))],
        out_specs=row(hv),
        scratch_shapes=[pltpu.VMEM((RET_HEADS, RET_KEY_DIM, RET_VAL_DIM), F32),
                        pltpu.VMEM((RET_HEADS, chunk, chunk), F32), pltpu.VMEM((RET_HEADS, chunk, LANES), F32),
                        pltpu.VMEM((RET_HEADS, chunk, LANES), F32)],
        compiler_params=_params("arbitrary"),
        name="ret_core",
    )(lg, q, k, v, g, gn_g.reshape(1, hv))


def _ret_fused_kernel(lg_ref, x_ref, mod_ref, ng_ref, cos_ref, sin_ref, w_ref, gn_ref, o_ref, state_ref, decay_ref,
                      xi_ref, zeta_ref, *, chunk, k_scale):
    @pl.when(pl.program_id(0) == 0)
    def _():
        state_ref[...] = jnp.zeros(state_ref.shape, F32)
        row = lax.broadcasted_iota(jnp.int32, (chunk, chunk), 0)
        col = lax.broadcasted_iota(jnp.int32, (chunk, chunk), 1)
        diff = (row - col).astype(F32)
        idx = lax.broadcasted_iota(jnp.int32, (chunk, LANES), 0).astype(F32)
        for hd in range(RET_HEADS):
            lg_row = lg_ref[hd][0:1, :]
            decay_ref[hd] = jnp.where(diff >= 0.0,
                                      jnp.exp(jnp.maximum(diff, 0.0) * _tile_lanes(lg_row, chunk)), 0.0)
            xi_ref[hd] = jnp.exp((idx + 1.0) * lg_row)
            zeta_ref[hd] = jnp.exp((chunk - 1.0 - idx) * lg_row)

    h = _modulated(x_ref[...], ng_ref[0:1, :], mod_ref[0:1, :], mod_ref[1:2, :])
    hk, hv = RET_HEADS * RET_KEY_DIM, RET_HEADS * RET_VAL_DIM
    cos, sin = cos_ref[...], sin_ref[...]
    half = RET_KEY_DIM // 2

    def rotated(t, scale):
        t1, t2 = t[:, :half], t[:, half:]
        return jnp.concatenate([((t1 * cos - t2 * sin) * scale).astype(BF16),
                                ((t2 * cos + t1 * sin) * scale).astype(BF16)], axis=1)

    for hd in range(RET_HEADS):
        ks = slice(hd * RET_KEY_DIM, (hd + 1) * RET_KEY_DIM)
        vs = slice(hd * RET_VAL_DIM, (hd + 1) * RET_VAL_DIM)
        q = rotated(_dot(h, w_ref[:, ks]), 1.0)
        k = rotated(_dot(h, w_ref[:, hk + hd * RET_KEY_DIM:hk + (hd + 1) * RET_KEY_DIM]), k_scale)
        v = _dot(h, w_ref[:, 2 * hk + hd * RET_VAL_DIM:2 * hk + (hd + 1) * RET_VAL_DIM]).astype(BF16)
        g = _dot(h, w_ref[:, 2 * hk + hv + hd * RET_VAL_DIM:2 * hk + hv + (hd + 1) * RET_VAL_DIM])
        decay, xi, zeta = decay_ref[hd], xi_ref[hd], zeta_ref[hd]
        g_chunk = jnp.exp(chunk * lg_ref[hd])
        state = state_ref[hd]
        sc = _dot_nt(q, k) * decay
        inner = _dot(sc.astype(BF16), v)
        cross = _dot(q, state.astype(BF16)) * _tile_lanes(xi, RET_VAL_DIM)
        kz = (k.astype(F32) * _tile_lanes(zeta, RET_KEY_DIM)).astype(BF16)
        upd = lax.dot_general(kz, v, (((0,), (0,)), ((), ())), preferred_element_type=F32)
        state_ref[hd] = state * _tile_lanes(g_chunk[0:1, :], RET_VAL_DIM) + upd
        o = inner + cross
        o = o * lax.rsqrt(jnp.mean(o * o, axis=-1, keepdims=True) + EPS)
        o = o * gn_ref[:, vs]
        o_ref[:, vs] = (g * (1.0 / (1.0 + jnp.exp(-g))) * o).astype(o_ref.dtype)


def _ret_mixer(x, mod, ng, cos, sin, w_in, gn_g):
    s, d = x.shape
    chunk = RET_CHUNK
    hv = RET_HEADS * RET_VAL_DIM
    log_gamma = jnp.log(1.0 - 2.0 ** (-5.0 - jnp.arange(RET_HEADS, dtype=F32)))
    lg = jnp.broadcast_to(log_gamma[:, None, None], (RET_HEADS, SUBLANES, LANES))
    row = lambda w: pl.BlockSpec((chunk, w), lambda i: (i, 0))
    w = w_in.astype(BF16)
    return pl.pallas_call(
        functools.partial(_ret_fused_kernel, chunk=chunk, k_scale=RET_KEY_DIM ** -0.5),
        out_shape=jax.ShapeDtypeStruct((s, hv), BF16),
        grid=(s // chunk,),
        in_specs=[_full(lg.shape), row(d), _full(mod.shape), _full(ng.shape), row(LANES), row(LANES),
                  pl.BlockSpec(w.shape, lambda i: (0, 0), pipeline_mode=pl.Buffered(1)), _full((1, hv))],
        out_specs=row(hv),
        scratch_shapes=[pltpu.VMEM((RET_HEADS, RET_KEY_DIM, RET_VAL_DIM), F32),
                        pltpu.VMEM((RET_HEADS, chunk, chunk), F32), pltpu.VMEM((RET_HEADS, chunk, LANES), F32),
                        pltpu.VMEM((RET_HEADS, chunk, LANES), F32)],
        compiler_params=_params("arbitrary"),
        name="ret_fused",
    )(lg, x, mod, ng, cos, sin, w, gn_g.reshape(1, hv))


def _post_kernel(o_ref, x_ref, mod_ref, ng_ref, wo_ref, w1_ref, w2_ref, out_ref, y_ref):
    x = x_ref[...]
    y = _dot(o_ref[...], wo_ref[...])
    x = x + mod_ref[2:3, :] * _rms(y, ng_ref[1:2, :])
    h = _modulated(x, ng_ref[2:3, :], mod_ref[3:4, :], mod_ref[4:5, :])
    d_ff = w1_ref.shape[1]
    for c in range(d_ff // FF_CHUNK):
        a = jnp.maximum(_dot(h, w1_ref[:, c * FF_CHUNK:(c + 1) * FF_CHUNK]), 0.0)
        part = _dot((a * a).astype(BF16), w2_ref[c * FF_CHUNK:(c + 1) * FF_CHUNK, :])
        if c == 0:
            y_ref[...] = part
        else:
            y_ref[...] += part
    out_ref[...] = x + mod_ref[5:6, :] * _rms(y_ref[...], ng_ref[3:4, :])


def _post(o, x, mod, ng, w_out, w1_all, w2_all, layer):
    s, d = x.shape
    tm = POST_ROW_TILE
    ko = o.shape[1]
    row = lambda w: pl.BlockSpec((tm, w), lambda i: (i, 0))
    resident = lambda a: pl.BlockSpec(a.shape, lambda i: (0, 0), pipeline_mode=pl.Buffered(1))
    of_layer = lambda a: pl.BlockSpec((None,) + a.shape[1:], lambda i: (layer, 0, 0), pipeline_mode=pl.Buffered(1))
    wo = w_out.astype(BF16)
    return pl.pallas_call(
        _post_kernel,
        out_shape=jax.ShapeDtypeStruct((s, d), F32),
        grid=(s // tm,),
        in_specs=[row(ko), row(d), _full(mod.shape), _full(ng.shape), resident(wo), of_layer(w1_all),
                  of_layer(w2_all)],
        out_specs=row(d),
        scratch_shapes=[pltpu.VMEM((tm, d), F32)],
        compiler_params=_params("arbitrary"),
        name="post",
    )(o, x, mod, ng, wo, w1_all, w2_all)


def kernel(x, c, positions, ada_w, ada_b, norm_g, ffn_w1, ffn_w2, mla_w_in, mla_q_norm, mla_kv_norm, mla_w_q_up,
           mla_w_kv_up, mla_w_out, sb_w_in, sb_w_out, ret_w_in, ret_gn_g, ret_w_out):
    batch, seq, d = x.shape
    assert batch == 1 and seq % MLA_Q_BLOCK == 0 and seq % ROW_TILE == 0 and seq % POST_ROW_TILE == 0
    depth = ada_w.shape[0]
    mod = _adaln(c, ada_w, ada_b)
    tables = _rope_tables(positions)
    w1_all, w2_all = ffn_w1.astype(BF16), ffn_w2.astype(BF16)
    xs = x.reshape(seq, d)
    for i in range(depth):
        kind, j = i % 3, i // 3
        if kind == 0:
            o = _mla_mixer(xs, mod[i], norm_g[i], tables, mla_w_in[j], mla_q_norm[j], mla_kv_norm[j],
                           mla_w_q_up[j], mla_w_kv_up[j])
            w_out = mla_w_out[j]
        elif kind == 1:
            q, k, v = _sb_in(xs, mod[i], norm_g[i], sb_w_in[j])
            o, w_out = _sb_attn(q, k, v), sb_w_out[j]
        else:
            o = _ret_mixer(xs, mod[i], norm_g[i], tables[2], tables[3], ret_w_in[j], ret_gn_g[j])
            w_out = ret_w_out[j]
        xs = _post(o, xs, mod[i], norm_g[i], w_out, w1_all, w2_all, i)
    return xs.reshape(batch, seq, d)
```
